```python
import jax
import jax.numpy as jnp
from jax import lax
import numpy as np

D_MODEL = 1024
BATCH = 2
SEQ = 8192
DEPTH = 4
DEC_BATCH = 128
DEC_SEQ = 1
PAST_LEN = 8192
PAGE_SIZE = 128

HEAD_DIM = 64
ROT_DIM = HEAD_DIM // 4
ROPE_THETA = 500000.0
Q_BLOCK = 128
RMS_EPS = 1e-6
A_HEADS = D_MODEL // HEAD_DIM
A_KV = 2
A_GROUP = A_HEADS // A_KV
CMP_BLOCK = 32
CMP_STRIDE = 16
CMP_RATIO = CMP_BLOCK // CMP_STRIDE
SEL_BLOCK = 64
N_SEL = 16
A_WINDOW = 512
FORCE_BONUS = 1000.0
B_HEADS = D_MODEL // HEAD_DIM
B_KV = 2
B_GROUP = B_HEADS // B_KV
B_WINDOW = 128
D_FF = -(-8 * D_MODEL // (3 * 256)) * 256

N_A_LAYERS = (DEPTH + 1) // 2
N_B_LAYERS = DEPTH // 2
A_QD = A_HEADS * HEAD_DIM
A_KVD = 2 * A_KV * HEAD_DIM
A_IN = A_QD + 3 * A_KVD + 3 * A_HEADS
B_QD = B_HEADS * HEAD_DIM
B_KVD = 2 * B_KV * HEAD_DIM
B_IN = B_QD + B_KVD
ATTN_SCALE = HEAD_DIM ** -0.5

kernel_name = 'nsa_swa_sink_hybrid_step'


def rmsnorm(x, g):
    xf = x.astype(jnp.float32)
    y = xf * lax.rsqrt(jnp.mean(xf * xf, axis=-1, keepdims=True) + RMS_EPS)
    return (y * g.astype(jnp.float32)).astype(x.dtype)


def rope(x, pos):
    half = ROT_DIM // 2
    inv = ROPE_THETA ** (-jnp.arange(half, dtype=jnp.float32) * (2.0 / ROT_DIM))
    ang = pos.astype(jnp.float32)[:, None] * inv[None, :]
    cos, sin = jnp.cos(ang)[:, None, :], jnp.sin(ang)[:, None, :]
    xf = x.astype(jnp.float32)
    x1, x2 = xf[..., :half], xf[..., half:ROT_DIM]
    out = jnp.concatenate([x1 * cos - x2 * sin, x2 * cos + x1 * sin, xf[..., ROT_DIM:]], axis=-1)
    return out.astype(x.dtype)


def rope_kv(kv, pos):
    return jnp.stack([rope(kv[:, :, 0], pos), kv[:, :, 1]], axis=2)


def masked_softmax(s, mask, sink=None):
    s = jnp.where(mask, s, -jnp.inf)
    m = jnp.max(s, axis=-1, keepdims=True)
    if sink is not None:
        m = jnp.maximum(m, sink)
    m = jnp.where(jnp.isfinite(m), m, 0.0)
    e = jnp.exp(s - m)
    den = jnp.sum(e, axis=-1, keepdims=True)
    if sink is not None:
        den = den + jnp.exp(sink - m)
    return e / jnp.where(den > 0, den, 1.0)


def attend(q, k, v, mask, sink=None):
    s = jnp.einsum('...qgmd,...kgd->...gmqk', q, k).astype(jnp.float32) * ATTN_SCALE
    p = masked_softmax(s, mask, sink)
    o = jnp.einsum('...gmqk,...kgd->...qgmd', p.astype(v.dtype), v)
    return o, p


def swiglu(x, w_gu, w_down):
    g, u = jnp.split(x @ w_gu, 2, axis=-1)
    return (jax.nn.silu(g) * u) @ w_down


def chunk_proj(rows, w):
    b, l = rows.shape[:2]
    n = l // CMP_STRIDE
    c = rows[:, :n * CMP_STRIDE].reshape(b, n, CMP_STRIDE, rows.shape[2], HEAD_DIM)
    w_r = w.reshape(CMP_RATIO, CMP_STRIDE, HEAD_DIM, HEAD_DIM)
    return jnp.einsum('bnsgd,rsde->rbnge', c, w_r)


def compress(proj, w, pe):
    nc = proj.shape[2] - CMP_RATIO + 1
    out = proj[0, :, :nc]
    for r in range(1, CMP_RATIO):
        out = out + proj[r, :, r:r + nc]
    return out + pe.reshape(-1) @ w


def overlap_matrix(nc, ns):
    c0 = jnp.arange(nc)[:, None] * CMP_STRIDE
    s0 = jnp.arange(ns)[None, :] * SEL_BLOCK
    return ((c0 < s0 + SEL_BLOCK) & (c0 + CMP_BLOCK > s0)).astype(jnp.float32)


def gather_rows(full, pos):
    b_idx = jnp.arange(full.shape[0])[:, None, None, None]
    g_idx = jnp.arange(full.shape[3])[None, :, None, None]
    return full[b_idx, jnp.clip(pos, 0, full.shape[1] - 1), :, g_idx]


def gather_paged_rows(pool, page_table, new, pos, past_len):
    b_idx = jnp.arange(page_table.shape[0])[:, None, None, None]
    g_idx = jnp.arange(pool.shape[3])[None, :, None, None]
    pp = jnp.clip(pos, 0, past_len - 1)
    page = page_table[b_idx, pp // PAGE_SIZE]
    past = pool[page, pp % PAGE_SIZE, :, g_idx]
    fresh = new[b_idx, jnp.clip(pos - past_len, 0, new.shape[1] - 1), :, g_idx]
    return jnp.where((pos < past_len)[..., None, None], past, fresh)


def nsa_attend(q_nr, q_rot, q_pos, kc, vc, gather_sel, n_sel_blocks, kw, vw, w_pos, gates):
    b, tq = q_rot.shape[:2]
    nc = kc.shape[1]
    cmp_end = jnp.arange(nc) * CMP_STRIDE + (CMP_BLOCK - 1)
    o_cmp, p_cmp = attend(q_nr, kc, vc, cmp_end[None, :] <= q_pos[:, None])
    imp = jnp.einsum('bgmqc,cj->bgqj', p_cmp, overlap_matrix(nc, n_sel_blocks))
    blk = jnp.arange(n_sel_blocks)[None, :]
    q_blk = (q_pos // SEL_BLOCK)[:, None]
    forced = (blk == 0) | (blk == q_blk) | (blk == q_blk - 1)
    score = jnp.where(blk * SEL_BLOCK <= q_pos[:, None],
                      imp + jnp.where(forced, FORCE_BONUS, 0.0), -jnp.inf)
    _, idx = lax.top_k(score, min(N_SEL, n_sel_blocks))
    pos = (idx[..., None] * SEL_BLOCK + jnp.arange(SEL_BLOCK)).reshape(idx.shape[:3] + (-1,))
    ks, vs = gather_sel(pos)
    s = jnp.einsum('bqgmd,bgqnd->bgmqn', q_rot, ks).astype(jnp.float32) * ATTN_SCALE
    p_sel = masked_softmax(s, (pos <= q_pos[:, None])[:, :, None])
    o_sel = jnp.einsum('bgmqn,bgqnd->bqgmd', p_sel.astype(vs.dtype), vs)
    dist = q_pos[:, None] - w_pos[None, :]
    o_win, _ = attend(q_rot, kw, vw, (dist >= 0) & (dist <= A_WINDOW) & (w_pos >= 0)[None, :])
    g = jax.nn.sigmoid(gates.astype(jnp.float32)).astype(o_win.dtype).reshape(b, tq, 3, A_KV, A_GROUP, 1)
    o = g[:, :, 0] * o_cmp + g[:, :, 1] * o_sel + g[:, :, 2] * o_win
    return o.reshape(b, tq, A_QD)


def split_a(p):
    b, t = p.shape[:2]
    q = p[..., :A_QD].reshape(b, t, A_HEADS, HEAD_DIM)
    kvs = [p[..., A_QD + c * A_KVD:A_QD + (c + 1) * A_KVD].reshape(b, t, 2, A_KV, HEAD_DIM) for c in range(3)]
    gates = p[..., A_QD + 3 * A_KVD:].reshape(b, t, 3, A_HEADS)
    return q, kvs[0], kvs[1], kvs[2], gates


def split_b(p):
    b, t = p.shape[:2]
    return p[..., :B_QD].reshape(b, t, B_HEADS, HEAD_DIM), p[..., B_QD:].reshape(b, t, 2, B_KV, HEAD_DIM)


def nsa_prompt(h, w_in, w_out, cmp_w, cmp_pe):
    b, s, _ = h.shape
    pos = jnp.arange(s)
    q, kv_c, kv_s, kv_w, gates = split_a(h @ w_in)
    q_nr = q.reshape(b, s, A_KV, A_GROUP, HEAD_DIM)
    q_rot = rope(q, pos).reshape(b, s, A_KV, A_GROUP, HEAD_DIM)
    kv_s = rope_kv(kv_s, pos)
    kv_w = rope_kv(kv_w, pos)
    kc = compress(chunk_proj(kv_c[:, :, 0], cmp_w[0]), cmp_w[0], cmp_pe[0])
    vc = compress(chunk_proj(kv_c[:, :, 1], cmp_w[1]), cmp_w[1], cmp_pe[1])
    n_sel_blocks = -(-s // SEL_BLOCK)
    kw_pad = jnp.pad(kv_w, ((0, 0), (A_WINDOW, 0), (0, 0), (0, 0), (0, 0)))

    def gather_sel(p):
        kv = gather_rows(kv_s, p)
        return kv[..., 0, :], kv[..., 1, :]

    def block(i):
        start = i * Q_BLOCK
        q_pos = start + jnp.arange(Q_BLOCK)
        sl = lambda t: lax.dynamic_slice_in_dim(t, start, Q_BLOCK, axis=1)
        kw_b = lax.dynamic_slice_in_dim(kw_pad, start, A_WINDOW + Q_BLOCK, axis=1)
        w_pos = start - A_WINDOW + jnp.arange(A_WINDOW + Q_BLOCK)
        return nsa_attend(sl(q_nr), sl(q_rot), q_pos, kc, vc, gather_sel, n_sel_blocks,
                          kw_b[:, :, 0], kw_b[:, :, 1], w_pos, sl(gates))

    o = lax.map(block, jnp.arange(s // Q_BLOCK))
    o = jnp.moveaxis(o, 0, 1).reshape(b, s, A_QD) @ w_out
    return o, kv_c, kv_s, kv_w[:, s - min(A_WINDOW, s):]


def nsa_sample(h, pool_c, pool_s, win, page_table, w_in, w_out, cmp_w, cmp_pe):
    b, t, _ = h.shape
    past_len = page_table.shape[1] * PAGE_SIZE
    pos = past_len + jnp.arange(t)
    q, kv_c, kv_s, kv_w, gates = split_a(h @ w_in)
    q_nr = q.reshape(b, t, A_KV, A_GROUP, HEAD_DIM)
    q_rot = rope(q, pos).reshape(b, t, A_KV, A_GROUP, HEAD_DIM)
    kv_s = rope_kv(kv_s, pos)
    kv_w = rope_kv(kv_w, pos)
    past_c = pool_c[page_table].reshape(b, past_len, 2, A_KV, HEAD_DIM)

    def cmp_summary(c):
        proj = jnp.concatenate([chunk_proj(past_c[:, :, c], cmp_w[c]),
                                chunk_proj(kv_c[:, :, c], cmp_w[c])], axis=2)
        return compress(proj, cmp_w[c], cmp_pe[c])

    kc, vc = cmp_summary(0), cmp_summary(1)
    n_sel_blocks = -(-(past_len + t) // SEL_BLOCK)

    def gather_sel(p):
        kv = gather_paged_rows(pool_s, page_table, kv_s, p, past_len)
        return kv[..., 0, :], kv[..., 1, :]

    kw_all = jnp.concatenate([win, kv_w], axis=1)
    wb = win.shape[1]
    w_pos = past_len - wb + jnp.arange(wb + t)
    o = nsa_attend(q_nr, q_rot, pos, kc, vc, gather_sel, n_sel_blocks,
                   kw_all[:, :, 0], kw_all[:, :, 1], w_pos, gates) @ w_out
    return o, kv_c, kv_s, kw_all[:, t:]


def swa_prompt(h, w_in, w_out, sinks):
    b, s, _ = h.shape
    pos = jnp.arange(s)
    q, kv = split_b(h @ w_in)
    q = rope(q, pos)
    kv = rope_kv(kv, pos)
    nb = s // Q_BLOCK
    n_prev = -(-B_WINDOW // Q_BLOCK)
    kv_pad = jnp.pad(kv, ((0, 0), (n_prev * Q_BLOCK, 0), (0, 0), (0, 0), (0, 0)))
    band = jnp.concatenate([kv_pad[:, j * Q_BLOCK:j * Q_BLOCK + s].reshape(b, nb, Q_BLOCK, 2, B_KV, HEAD_DIM)
                            for j in range(n_prev + 1)], axis=2)
    q_pos = pos.reshape(nb, Q_BLOCK)
    k_pos = (jnp.arange(nb)[:, None] - n_prev) * Q_BLOCK + jnp.arange((n_prev + 1) * Q_BLOCK)[None, :]
    dist = q_pos[:, :, None] - k_pos[:, None, :]
    mask = (dist >= 0) & (dist <= B_WINDOW) & (k_pos >= 0)[:, None, :]
    sink = sinks.astype(jnp.float32).reshape(B_KV, B_GROUP, 1, 1)
    o, _ = attend(q.reshape(b, nb, Q_BLOCK, B_KV, B_GROUP, HEAD_DIM), band[:, :, :, 0], band[:, :, :, 1],
                  mask[:, None, None], sink)
    return o.reshape(b, s, B_QD) @ w_out, kv[:, s - min(B_WINDOW, s):]


def swa_sample(h, win, past_len, w_in, w_out, sinks):
    b, t, _ = h.shape
    pos = past_len + jnp.arange(t)
    q, kv = split_b(h @ w_in)
    q = rope(q, pos).reshape(b, t, B_KV, B_GROUP, HEAD_DIM)
    kv = rope_kv(kv, pos)
    kv_all = jnp.concatenate([win, kv], axis=1)
    wb = win.shape[1]
    k_pos = past_len - wb + jnp.arange(wb + t)
    dist = pos[:, None] - k_pos[None, :]
    mask = (dist >= 0) & (dist <= B_WINDOW) & (k_pos >= 0)[None, :]
    sink = sinks.astype(jnp.float32).reshape(B_KV, B_GROUP, 1, 1)
    o, _ = attend(q, kv_all[:, :, 0], kv_all[:, :, 1], mask, sink)
    return o.reshape(b, t, B_QD) @ w_out, kv_all[:, t:]


def setup_inputs(seed: int = 0) -> dict:
    key = jax.random.key(seed)
    ks = jax.random.split(key, 18)
    f32 = jnp.float32
    nrm = lambda k, shape, scale: jax.random.normal(k, shape, f32) * scale
    n_pages = PAST_LEN // PAGE_SIZE
    n_used = DEC_BATCH * n_pages
    n_phys = n_used + max(1, n_used // 4)
    page_table = jax.random.permutation(ks[2], n_phys)[:n_used].reshape(DEC_BATCH, n_pages).astype(jnp.int32)
    return {
        'x_prompt': nrm(ks[0], (BATCH, SEQ, D_MODEL), 1.0),
        'x_sample': nrm(ks[1], (DEC_BATCH, DEC_SEQ, D_MODEL), 1.0),
        'page_table': page_table,
        'cache_a_cmp': nrm(ks[3], (N_A_LAYERS, n_phys, PAGE_SIZE, 2, A_KV, HEAD_DIM), 1.0),
        'cache_a_sel': nrm(ks[4], (N_A_LAYERS, n_phys, PAGE_SIZE, 2, A_KV, HEAD_DIM), 1.0),
        'state_a_win': nrm(ks[5], (N_A_LAYERS, DEC_BATCH, min(A_WINDOW, PAST_LEN), 2, A_KV, HEAD_DIM), 1.0),
        'state_b_win': nrm(ks[6], (N_B_LAYERS, DEC_BATCH, min(B_WINDOW, PAST_LEN), 2, B_KV, HEAD_DIM), 1.0),
        'norms': 1.0 + nrm(ks[7], (DEPTH, 4, D_MODEL), 0.05),
        'ffn_w_gu': nrm(ks[8], (DEPTH, D_MODEL, 2 * D_FF), D_MODEL ** -0.5),
        'ffn_w_down': nrm(ks[9], (DEPTH, D_FF, D_MODEL), D_FF ** -0.5),
        'a_w_in': nrm(ks[10], (N_A_LAYERS, D_MODEL, A_IN), D_MODEL ** -0.5),
        'a_w_out': nrm(ks[11], (N_A_LAYERS, A_QD, D_MODEL), A_QD ** -0.5),
        'a_cmp_w': nrm(ks[12], (N_A_LAYERS, 2, CMP_BLOCK * HEAD_DIM, HEAD_DIM), (CMP_BLOCK * HEAD_DIM) ** -0.5),
        'a_cmp_pe': nrm(ks[13], (N_A_LAYERS, 2, CMP_BLOCK, HEAD_DIM), 0.5),
        'b_w_in': nrm(ks[14], (N_B_LAYERS, D_MODEL, B_IN), D_MODEL ** -0.5),
        'b_w_out': nrm(ks[15], (N_B_LAYERS, B_QD, D_MODEL), B_QD ** -0.5),
        'b_sinks': nrm(ks[16], (N_B_LAYERS, B_HEADS), 1.0),
    }


def reference(x_prompt, x_sample, page_table, cache_a_cmp, cache_a_sel, state_a_win, state_b_win,
              norms, ffn_w_gu, ffn_w_down, a_w_in, a_w_out, a_cmp_w, a_cmp_pe, b_w_in, b_w_out, b_sinks):
    past_len = page_table.shape[1] * PAGE_SIZE
    hp, hs = x_prompt, x_sample
    a_cmp_p, a_cmp_s, a_sel_p, a_sel_s, a_win_p, a_win_s, b_win_p, b_win_s = [], [], [], [], [], [], [], []
    for i in range(DEPTH):
        j = i // 2
        g_pre, g_post, f_pre, f_post = norms[i, 0], norms[i, 1], norms[i, 2], norms[i, 3]
        mp, ms = rmsnorm(hp, g_pre), rmsnorm(hs, g_pre)
        if i % 2 == 0:
            op, c_p, s_p, w_p = nsa_prompt(mp, a_w_in[j], a_w_out[j], a_cmp_w[j], a_cmp_pe[j])
            os_, c_s, s_s, w_s = nsa_sample(ms, cache_a_cmp[j], cache_a_sel[j], state_a_win[j], page_table,
                                            a_w_in[j], a_w_out[j], a_cmp_w[j], a_cmp_pe[j])
            a_cmp_p.append(c_p)
            a_cmp_s.append(c_s)
            a_sel_p.append(s_p)
            a_sel_s.append(s_s)
            a_win_p.append(w_p)
            a_win_s.append(w_s)
        else:
            op, w_p = swa_prompt(mp, b_w_in[j], b_w_out[j], b_sinks[j])
            os_, w_s = swa_sample(ms, state_b_win[j], past_len, b_w_in[j], b_w_out[j], b_sinks[j])
            b_win_p.append(w_p)
            b_win_s.append(w_s)
        hp = hp + rmsnorm(op, g_post)
        hs = hs + rmsnorm(os_, g_post)
        hp = hp + rmsnorm(swiglu(rmsnorm(hp, f_pre), ffn_w_gu[i], ffn_w_down[i]), f_post)
        hs = hs + rmsnorm(swiglu(rmsnorm(hs, f_pre), ffn_w_gu[i], ffn_w_down[i]), f_post)
    new_a_cmp_prompt = jnp.stack(a_cmp_p)
    new_a_cmp_sample = jnp.stack(a_cmp_s)
    new_a_sel_prompt = jnp.stack(a_sel_p)
    new_a_sel_sample = jnp.stack(a_sel_s)
    new_a_win_prompt = jnp.stack(a_win_p)
    new_a_win_sample = jnp.stack(a_win_s)
    new_b_win_prompt = jnp.stack(b_win_p)
    new_b_win_sample = jnp.stack(b_win_s)
    return (hp, hs, new_a_cmp_prompt, new_a_cmp_sample, new_a_sel_prompt, new_a_sel_sample,
            new_a_win_prompt, new_a_win_sample, new_b_win_prompt, new_b_win_sample)
```

```python
import functools

import jax
import jax.numpy as jnp
from jax import lax
from jax.experimental import pallas as pl
from jax.experimental.pallas import tpu as pltpu

F32 = jnp.float32
BF16 = jnp.bfloat16

HEAD_DIM = 64
ROT_DIM = HEAD_DIM // 4
ROPE_THETA = 500000.0
RMS_EPS = 1e-6
N_HEADS = 16
N_KV = 2
GROUP = N_HEADS // N_KV
KVD = 2 * N_KV * HEAD_DIM
CMP_BLOCK = 32
CMP_STRIDE = 16
CMP_RATIO = CMP_BLOCK // CMP_STRIDE
SEL_BLOCK = 64
N_SEL = 16
A_WINDOW = 512
B_WINDOW = 128
FORCE_BONUS = 1000.0
PAGE_SIZE = 128
ATTN_SCALE = HEAD_DIM ** -0.5
LANES = 128
Q_TILE = 128
SEL_KEY_TILE = 512
NEG = -1e30
PICKED = -3e30
VMEM_LIMIT = 56 * 1024 * 1024


def _cparams(*sem):
    return pltpu.CompilerParams(dimension_semantics=sem, vmem_limit_bytes=VMEM_LIMIT)


def _rms(x, g):
    return x * lax.rsqrt(jnp.mean(x * x, axis=-1, keepdims=True) + RMS_EPS) * g


def _nt(a, b):
    return lax.dot_general(a, b, (((1,), (1,)), ((), ())), preferred_element_type=F32)


def _nn(a, b):
    return jnp.dot(a, b, preferred_element_type=F32)


def _split3(x):
    hi = x.astype(BF16)
    r1 = x - hi.astype(F32)
    mid = r1.astype(BF16)
    lo = (r1 - mid.astype(F32)).astype(BF16)
    return hi, mid, lo


def _rope_tables(pos):
    half = ROT_DIM // 2
    inv = ROPE_THETA ** (-jnp.arange(half, dtype=F32) * (2.0 / ROT_DIM))
    ang = pos.astype(F32)[:, None] * inv[None, :]
    cos, sin = jnp.cos(ang), jnp.sin(ang)
    t = pos.shape[0]
    c64 = jnp.concatenate([cos, cos, jnp.ones((t, HEAD_DIM - ROT_DIM), F32)], axis=-1)
    up64 = jnp.concatenate([jnp.zeros((t, half), F32), sin, jnp.zeros((t, HEAD_DIM - ROT_DIM), F32)], axis=-1)
    dn64 = jnp.concatenate([-sin, jnp.zeros((t, HEAD_DIM - half), F32)], axis=-1)
    two = lambda a: jnp.concatenate([a, a], axis=-1)
    return two(c64), two(up64), two(dn64)


def _chunk_products(rows_ref, wbig_ref):
    n_chunks = rows_ref.shape[1] // CMP_STRIDE
    acc = [jnp.zeros((n_chunks, CMP_RATIO * LANES), F32) for _ in range(2)]
    for c in range(2):
        for s in range(CMP_STRIDE):
            rows = rows_ref[c, pl.ds(s, n_chunks, stride=CMP_STRIDE), :]
            acc[c] = acc[c] + _nn(rows.astype(BF16), wbig_ref[c, s])
    return jnp.concatenate([acc[c][:, r * LANES:(r + 1) * LANES] for r in range(CMP_RATIO) for c in range(2)], axis=-1)


def _proj_kernel(x_ref, g_ref, w_ref, c_ref, up_ref, dn_ref, wbig_ref, *refs, nsa):
    if nsa:
        (qn_ref, qr_ref, kvc_t_ref, kvs_t_ref, kvw_t_ref, ks_bf_ref, kw_bf_ref, gate_ref, cp_ref, kvc_scr) = refs
    else:
        (qr_ref, kv_t_ref, kv_bf_ref) = refs
    half = ROT_DIM // 2
    y = _rms(x_ref[...], g_ref[...])
    p = _nn(y.astype(BF16), w_ref[...])
    c, up, dn = c_ref[...], up_ref[...], dn_ref[...]

    def rope(v):
        return v * c + pltpu.roll(v, half, 1) * up + pltpu.roll(v, LANES - half, 1) * dn

    for j in range(N_HEADS // 2):
        ch = p[:, j * LANES:(j + 1) * LANES]
        rot = (rope(ch) * ATTN_SCALE).astype(BF16)
        for t in range(2):
            qr_ref[2 * j + t] = rot[:, t * HEAD_DIM:(t + 1) * HEAD_DIM]
        if nsa:
            raw = (ch * ATTN_SCALE).astype(BF16)
            for t in range(2):
                qn_ref[2 * j + t] = raw[:, t * HEAD_DIM:(t + 1) * HEAD_DIM]
    base = N_HEADS * HEAD_DIM

    def roped_kv(off):
        k = rope(p[:, off:off + LANES])
        v = p[:, off + LANES:off + KVD]
        return jnp.concatenate([k, v], axis=-1).T

    if nsa:
        kvc = p[:, base:base + KVD]
        kvc_t_ref[...] = kvc.T
        kvs_t = roped_kv(base + KVD)
        kvw_t = roped_kv(base + 2 * KVD)
        kvs_t_ref[...] = kvs_t
        kvw_t_ref[...] = kvw_t
        ks_bf_ref[...] = kvs_t.astype(BF16)
        kw_bf_ref[...] = kvw_t.astype(BF16)
        gate_ref[...] = jax.nn.sigmoid(p[:, base + 3 * KVD:base + 3 * KVD + LANES])
        for c in range(2):
            kvc_scr[c] = kvc[:, c * LANES:(c + 1) * LANES]
        cp_ref[...] = _chunk_products(kvc_scr, wbig_ref)
    else:
        kv_t = roped_kv(base)
        kv_t_ref[...] = kv_t
        kv_bf_ref[...] = kv_t.astype(BF16)


def _project(h, g_pre, w_pad, pos, wbig, *, nsa, tm):
    b, s, d = h.shape
    n = w_pad.shape[1]
    c, up, dn = _rope_tables(pos)
    grid = (b, s // tm)
    tok = lambda bb, i: (bb, i, 0)
    feat = lambda bb, i: (bb, 0, i)
    head = lambda bb, i: (bb, 0, i, 0)
    const2 = lambda bb, i: (0, 0)
    const3 = lambda bb, i: (0, 0, 0)
    tab = lambda bb, i: (i, 0)
    in_specs = [
        pl.BlockSpec((None, tm, d), tok),
        pl.BlockSpec((1, d), const2),
        pl.BlockSpec((d, n), const2),
        pl.BlockSpec((tm, LANES), tab),
        pl.BlockSpec((tm, LANES), tab),
        pl.BlockSpec((tm, LANES), tab),
        pl.BlockSpec(wbig.shape, lambda bb, i: (0,) * wbig.ndim),
    ]
    q_spec = pl.BlockSpec((None, N_HEADS, tm, HEAD_DIM), head)
    q_shape = jax.ShapeDtypeStruct((b, N_HEADS, s, HEAD_DIM), BF16)
    t_spec = pl.BlockSpec((None, KVD, tm), feat)
    t_f32 = jax.ShapeDtypeStruct((b, KVD, s), F32)
    t_bf = jax.ShapeDtypeStruct((b, KVD, s), BF16)
    if nsa:
        out_specs = [q_spec, q_spec, t_spec, t_spec, t_spec, t_spec, t_spec,
                     pl.BlockSpec((None, tm, LANES), tok),
                     pl.BlockSpec((None, tm // CMP_STRIDE, CMP_RATIO * KVD), tok)]
        out_shape = [q_shape, q_shape, t_f32, t_f32, t_f32, t_bf, t_bf,
                     jax.ShapeDtypeStruct((b, s, LANES), F32),
                     jax.ShapeDtypeStruct((b, s // CMP_STRIDE, CMP_RATIO * KVD), F32)]
        scratch = [pltpu.VMEM((2, tm, LANES), F32)]
    else:
        out_specs = [q_spec, t_spec, t_spec]
        out_shape = [q_shape, t_f32, t_bf]
        scratch = []
    return pl.pallas_call(
        functools.partial(_proj_kernel, nsa=nsa),
        grid=grid, in_specs=in_specs, out_specs=out_specs, out_shape=out_shape,
        scratch_shapes=scratch, compiler_params=_cparams("parallel", "parallel"),
        name="proj_nsa" if nsa else "proj_swa",
    )(h, g_pre.reshape(1, d), w_pad, c, up, dn, wbig)


def _post_kernel(h_ref, o_ref, wout_ref, gpost_ref, fpre_ref, fpost_ref, wgu_ref, wdown_ref, out_ref):
    d_ff = wdown_ref.shape[0]
    a = _nn(o_ref[...].astype(BF16), wout_ref[...])
    h = h_ref[...] + _rms(a, gpost_ref[...])
    x = _rms(h, fpre_ref[...]).astype(BF16)
    gu = _nn(x, wgu_ref[...])
    g, u = gu[:, :d_ff], gu[:, d_ff:]
    act = (g * jax.nn.sigmoid(g) * u).astype(BF16)
    out_ref[...] = h + _rms(_nn(act, wdown_ref[...]), fpost_ref[...])


def _post(h, o, w_out, g_post, f_pre, f_post, w_gu, w_down, *, tm):
    t, d = h.shape
    kd = o.shape[1]
    d_ff = w_down.shape[0]
    row = lambda i: (i, 0)
    const = lambda i: (0, 0)
    once = pl.Buffered(1)
    vec = lambda: pl.BlockSpec((1, d), const)
    return pl.pallas_call(
        _post_kernel,
        grid=(t // tm,),
        in_specs=[pl.BlockSpec((tm, d), row), pl.BlockSpec((tm, kd), row),
                  pl.BlockSpec((kd, d), const, pipeline_mode=once), vec(), vec(), vec(),
                  pl.BlockSpec((d, 2 * d_ff), const, pipeline_mode=once),
                  pl.BlockSpec((d_ff, d), const, pipeline_mode=once)],
        out_specs=pl.BlockSpec((tm, d), row),
        out_shape=jax.ShapeDtypeStruct((t, d), F32),
        compiler_params=_cparams("parallel"),
        name="post_ffn",
    )(h, o, w_out, g_post.reshape(1, d), f_pre.reshape(1, d), f_post.reshape(1, d), w_gu, w_down)


def _iota(shape, axis):
    return lax.broadcasted_iota(jnp.int32, shape, axis)


def _cmp_finish(cp, pe_ref, w_ref):
    ncp = cp.shape[0]
    out = []
    for c in range(2):
        pew = _nn(pe_ref[c], w_ref[c])[0:1]
        first = cp[:, c * LANES:(c + 1) * LANES]
        second = pltpu.roll(cp[:, KVD + c * LANES:KVD + (c + 1) * LANES], ncp - 1, 0)
        out.append(first + second + jnp.concatenate([pew, pew], axis=-1))
    return out


def _cmpfin_kernel(cp_ref, pe_ref, w_ref, out_ref):
    for c, both in enumerate(_cmp_finish(cp_ref[...], pe_ref, w_ref)):
        for g in range(N_KV):
            out_ref[c, g] = both[:, g * HEAD_DIM:(g + 1) * HEAD_DIM].astype(BF16)


def _cmpfin(cp, pe8, cmp_w_bf):
    b, ncp, _ = cp.shape
    return pl.pallas_call(
        _cmpfin_kernel,
        grid=(b,),
        in_specs=[pl.BlockSpec((None, ncp, CMP_RATIO * KVD), lambda i: (i, 0, 0)),
                  pl.BlockSpec(pe8.shape, lambda i: (0, 0, 0)),
                  pl.BlockSpec(cmp_w_bf.shape, lambda i: (0, 0, 0))],
        out_specs=pl.BlockSpec((None, 2, N_KV, ncp, HEAD_DIM), lambda i: (i, 0, 0, 0, 0)),
        out_shape=jax.ShapeDtypeStruct((b, 2, N_KV, ncp, HEAD_DIM), BF16),
        compiler_params=_cparams("parallel"),
        name="cmp_finish",
    )(cp, pe8, cmp_w_bf)


def _online_tile(q, k_t, v_t, bias, carry):
    m, l, acc = carry
    tk = k_t.shape[1]
    s = _nn(q, k_t).reshape(GROUP, Q_TILE, tk) + bias[None]
    m_new = jnp.maximum(m, jnp.max(s, axis=-1, keepdims=True))
    alpha = jnp.exp(m - m_new)
    p = jnp.exp(s - m_new)
    l = alpha * l + jnp.sum(p, axis=-1, keepdims=True)
    pv = _nt(p.reshape(GROUP * Q_TILE, tk).astype(BF16), v_t).reshape(GROUP, Q_TILE, HEAD_DIM)
    return m_new, l, alpha * acc + pv


def _online_init():
    return (jnp.full((GROUP, Q_TILE, 1), NEG, F32), jnp.zeros((GROUP, Q_TILE, 1), F32),
            jnp.zeros((GROUP, Q_TILE, HEAD_DIM), F32))


def _window_sweep(q, k_ref, v_ref, i, window):
    start = i * Q_TILE

    def step(j, carry):
        k0 = pl.multiple_of(j * Q_TILE, Q_TILE)
        dist = (start + _iota((Q_TILE, Q_TILE), 0)) - (k0 + _iota((Q_TILE, Q_TILE), 1))
        bias = jnp.where((dist >= 0) & (dist <= window), 0.0, NEG)
        return _online_tile(q, k_ref[:, pl.ds(k0, Q_TILE)], v_ref[:, pl.ds(k0, Q_TILE)], bias, carry)

    return lax.fori_loop(jnp.maximum(i - window // Q_TILE, 0), i + 1, step, _online_init())


def _nsa_attn_kernel(qn_ref, qr_ref, kc_ref, vc_ref, ks_ref, vs_ref, kw_ref, vw_ref, gate_ref, o_ref, *, n_pick):
    g = pl.program_id(1)
    i = pl.program_id(2)
    start = i * Q_TILE
    rows = GROUP * Q_TILE
    ncp = kc_ref.shape[0]
    qn = qn_ref[...].reshape(rows, HEAD_DIM)
    qr = qr_ref[...].reshape(rows, HEAD_DIM)

    s = _nt(qn, kc_ref[...]).reshape(GROUP, Q_TILE, ncp)
    cend = _iota((Q_TILE, ncp), 1) * CMP_STRIDE + (CMP_BLOCK - 1)
    cmask = (cend <= start + _iota((Q_TILE, ncp), 0))[None]
    s = jnp.where(cmask, s, NEG)
    m = jnp.max(s, axis=-1, keepdims=True)
    e = jnp.where(cmask, jnp.exp(s - m), 0.0)
    den = jnp.sum(e, axis=-1, keepdims=True)
    p = e / jnp.where(den > 0, den, 1.0)
    o_cmp = _nn(p.reshape(rows, ncp).astype(BF16), vc_ref[...]).reshape(GROUP, Q_TILE, HEAD_DIM)

    c0 = _iota((ncp, LANES), 0) * CMP_STRIDE
    s0 = _iota((ncp, LANES), 1) * SEL_BLOCK
    overlap = jnp.where((c0 < s0 + SEL_BLOCK) & (c0 + CMP_BLOCK > s0), 1.0, 0.0).astype(BF16)
    imp = sum(_nn(t, overlap) for t in _split3(jnp.sum(p, axis=0)))
    lane = _iota((Q_TILE, LANES), 1)
    qpos = start + _iota((Q_TILE, LANES), 0)
    qblk = qpos // SEL_BLOCK
    forced = (lane == 0) | (lane == qblk) | (lane == qblk - 1)
    score = jnp.where(lane * SEL_BLOCK <= qpos, imp + jnp.where(forced, FORCE_BONUS, 0.0), NEG)
    sel = jnp.zeros((Q_TILE, LANES), F32)
    for _ in range(n_pick):
        best = jnp.max(score, axis=-1, keepdims=True)
        idx = jnp.min(jnp.where(score == best, lane, LANES), axis=-1, keepdims=True)
        pick = lane == idx
        sel = jnp.where(pick, 1.0, sel)
        score = jnp.where(pick, PICKED, score)
    sel_bf = sel.astype(BF16)

    tk = SEL_KEY_TILE

    def sel_step(kt, carry):
        k0 = pl.multiple_of(kt * tk, tk)
        blk = _iota((LANES, tk), 1) // SEL_BLOCK + kt * (tk // SEL_BLOCK)
        expand = jnp.where(_iota((LANES, tk), 0) == blk, 1.0, 0.0).astype(BF16)
        chosen = _nn(sel_bf, expand)
        causal = (k0 + _iota((Q_TILE, tk), 1)) <= (start + _iota((Q_TILE, tk), 0))
        bias = jnp.where((chosen > 0.5) & causal, 0.0, NEG)
        return _online_tile(qr, ks_ref[:, pl.ds(k0, tk)], vs_ref[:, pl.ds(k0, tk)], bias, carry)

    _, l_sel, acc_sel = lax.fori_loop(0, (start + Q_TILE + tk - 1) // tk, sel_step, _online_init())
    o_sel = acc_sel / l_sel

    _, l_win, acc_win = _window_sweep(qr, kw_ref, vw_ref, i, A_WINDOW)
    o_win = acc_win / l_win

    gt = gate_ref[...]

    def gate(branch, head):
        col = branch * N_HEADS + head
        return jnp.where(g == 0, gt[:, col:col + 1], gt[:, col + GROUP:col + GROUP + 1])

    heads = [gate(0, h) * o_cmp[h] + gate(1, h) * o_sel[h] + gate(2, h) * o_win[h] for h in range(GROUP)]
    o_ref[...] = jnp.concatenate(heads, axis=-1).astype(o_ref.dtype)


def _nsa_attention(qn, qr, kcv, ks_bf, kw_bf, gates):
    b, _, s, _ = qn.shape
    ncp = kcv.shape[3]
    assert s % SEL_KEY_TILE == 0 and s // SEL_BLOCK <= LANES
    q_spec = pl.BlockSpec((None, GROUP, Q_TILE, HEAD_DIM), lambda bb, g, i: (bb, g, i, 0))
    cmp_spec = lambda c: pl.BlockSpec((None, None, None, ncp, HEAD_DIM), lambda bb, g, i: (bb, c, g, 0, 0))
    slab = lambda c: pl.BlockSpec((None, HEAD_DIM, s), lambda bb, g, i: (bb, c * N_KV + g, 0))
    return pl.pallas_call(
        functools.partial(_nsa_attn_kernel, n_pick=min(N_SEL, s // SEL_BLOCK)),
        grid=(b, N_KV, s // Q_TILE),
        in_specs=[q_spec, q_spec, cmp_spec(0), cmp_spec(1), slab(0), slab(1), slab(0), slab(1),
                  pl.BlockSpec((None, Q_TILE, LANES), lambda bb, g, i: (bb, i, 0))],
        out_specs=pl.BlockSpec((None, Q_TILE, GROUP * HEAD_DIM), lambda bb, g, i: (bb, i, g)),
        out_shape=jax.ShapeDtypeStruct((b, s, N_HEADS * HEAD_DIM), BF16),
        compiler_params=_cparams("parallel", "parallel", "arbitrary"),
        name="nsa_attn",
    )(qn, qr, kcv, kcv, ks_bf, ks_bf, kw_bf, kw_bf, gates)


def _swa_attn_kernel(qr_ref, k_ref, v_ref, sink_ref, o_ref):
    i = pl.program_id(2)
    q = qr_ref[...].reshape(GROUP * Q_TILE, HEAD_DIM)
    m, l, acc = _window_sweep(q, k_ref, v_ref, i, B_WINDOW)
    sink = sink_ref[...]
    m_f = jnp.maximum(m, sink)
    scale = jnp.exp(m - m_f)
    out = acc * scale / (l * scale + jnp.exp(sink - m_f))
    o_ref[...] = jnp.concatenate([out[h] for h in range(GROUP)], axis=-1).astype(o_ref.dtype)


def _swa_attention(qr, kv_bf, sinks):
    b, _, s, _ = qr.shape
    sink = jnp.broadcast_to(sinks.astype(F32).reshape(N_KV, GROUP, 1, 1), (N_KV, GROUP, Q_TILE, 1))
    slab = lambda c: pl.BlockSpec((None, HEAD_DIM, s), lambda bb, g, i: (bb, c * N_KV + g, 0))
    return pl.pallas_call(
        _swa_attn_kernel,
        grid=(b, N_KV, s // Q_TILE),
        in_specs=[pl.BlockSpec((None, GROUP, Q_TILE, HEAD_DIM), lambda bb, g, i: (bb, g, i, 0)),
                  slab(0), slab(1),
                  pl.BlockSpec((None, GROUP, Q_TILE, 1), lambda bb, g, i: (g, 0, 0, 0))],
        out_specs=pl.BlockSpec((None, Q_TILE, GROUP * HEAD_DIM), lambda bb, g, i: (bb, i, g)),
        out_shape=jax.ShapeDtypeStruct((b, s, N_HEADS * HEAD_DIM), BF16),
        compiler_params=_cparams("parallel", "parallel", "arbitrary"),
        name="swa_attn",
    )(qr, kv_bf, kv_bf, sink)


def _own_value_lanes():
    row_g = _iota((N_HEADS, KVD), 0) // GROUP
    lane = _iota((N_HEADS, KVD), 1)
    return (lane >= KVD // 2) & ((lane - KVD // 2) // HEAD_DIM == row_g)


def _attend_with_self(q, s, k_t_bf_list, row_sel, self_row, sink=None):
    s_self = jnp.sum(q.astype(F32) * self_row, axis=-1, keepdims=True)
    m = jnp.maximum(jnp.max(s, axis=-1, keepdims=True), s_self)
    if sink is not None:
        m = jnp.maximum(m, sink)
    p = jnp.exp(s - m)
    p_self = jnp.exp(s_self - m)
    den = jnp.sum(p, axis=-1, keepdims=True) + p_self
    if sink is not None:
        den = den + jnp.exp(sink - m)
    p_bf = p.astype(BF16)
    pv = _nt(p_bf, k_t_bf_list[0])
    if len(k_t_bf_list) > 1:
        pv = jnp.where(row_sel, pv, _nt(p_bf, k_t_bf_list[1]))
    return (pv + p_self * self_row) / den


def _cmp_sample_kernel(pt_ref, qc_ref, pool_ref, wbig_ref, pe_ref, w_ref, o_ref, idx_ref, buf, xs, sem,
                       *, layer, n_pages, past_len, n_pick):
    b = pl.program_id(0)
    slot = b % 2

    def fetch(bb, sl):
        def body(p, carry):
            page = pt_ref[bb * n_pages + p]
            pltpu.make_async_copy(pool_ref.at[layer, page], buf.at[sl, p], sem.at[sl]).start()
            return carry
        lax.fori_loop(0, n_pages, body, 0)

    @pl.when(b == 0)
    def _():
        fetch(0, 0)

    @pl.when(b + 1 < pl.num_programs(0))
    def _():
        fetch(b + 1, 1 - slot)

    def wait(p, carry):
        pltpu.make_async_copy(pool_ref.at[layer, 0], buf.at[slot, p], sem.at[slot]).wait()
        return carry
    lax.fori_loop(0, n_pages, wait, 0)

    def flip(p, carry):
        r0 = pl.multiple_of(p * PAGE_SIZE, PAGE_SIZE)
        for c in range(2):
            xs[c, pl.ds(r0, PAGE_SIZE), :] = buf[slot, p, c * LANES:(c + 1) * LANES, :].T
        return carry
    lax.fori_loop(0, n_pages, flip, 0)
    n_chunks = n_pages * PAGE_SIZE // CMP_STRIDE
    kc, vc = _cmp_finish(_chunk_products(xs, wbig_ref), pe_ref, w_ref)

    q = qc_ref[...]
    s = _nt(q, kc.astype(BF16))
    cmask = _iota((N_HEADS, n_chunks), 1) * CMP_STRIDE + (CMP_BLOCK - 1) <= past_len
    s = jnp.where(cmask, s, NEG)
    m = jnp.max(s, axis=-1, keepdims=True)
    e = jnp.where(cmask, jnp.exp(s - m), 0.0)
    den = jnp.sum(e, axis=-1, keepdims=True)
    p = e / jnp.where(den > 0, den, 1.0)
    o = _nn(p.astype(BF16), vc.astype(BF16))
    own = _iota((N_HEADS, LANES), 0) // GROUP == _iota((N_HEADS, LANES), 1) // HEAD_DIM
    o_ref[...] = jnp.concatenate([jnp.zeros((N_HEADS, LANES), F32), jnp.where(own, o, 0.0)], axis=-1)

    c0 = _iota((n_chunks, LANES), 0) * CMP_STRIDE
    s0 = _iota((n_chunks, LANES), 1) * SEL_BLOCK
    overlap = jnp.where((c0 < s0 + SEL_BLOCK) & (c0 + CMP_BLOCK > s0), 1.0, 0.0).astype(BF16)
    imp_h = sum(_nn(t, overlap) for t in _split3(p))
    imp = jnp.concatenate([jnp.sum(imp_h[g * GROUP:(g + 1) * GROUP], axis=0, keepdims=True) for g in range(N_KV)], axis=0)
    lane = _iota((N_KV, LANES), 1)
    qblk = past_len // SEL_BLOCK
    forced = (lane == 0) | (lane == qblk) | (lane == qblk - 1)
    score = jnp.where(lane * SEL_BLOCK < past_len, imp + jnp.where(forced, FORCE_BONUS, 0.0), NEG)
    picks = jnp.zeros((N_KV, LANES), jnp.int32)
    for it in range(n_pick):
        best = jnp.max(score, axis=-1, keepdims=True)
        idx = jnp.min(jnp.where(score == best, lane, LANES), axis=-1, keepdims=True)
        score = jnp.where(lane == idx, PICKED, score)
        picks = jnp.where(lane == it, idx, picks)
    idx_ref[...] = picks


def _cmp_sample(pt_flat, qc_pad, pool, wbig, pe8, cmp_w_bf, *, layer, n_pages, n_pick):
    nb = qc_pad.shape[0]
    past_len = n_pages * PAGE_SIZE
    n_chunks = past_len // CMP_STRIDE
    assert past_len // SEL_BLOCK <= LANES
    const3 = lambda i, pt: (0, 0, 0)
    grid_spec = pltpu.PrefetchScalarGridSpec(
        num_scalar_prefetch=1, grid=(nb,),
        in_specs=[pl.BlockSpec((None, N_HEADS, LANES), lambda i, pt: (i, 0, 0)),
                  pl.BlockSpec(memory_space=pl.ANY),
                  pl.BlockSpec(wbig.shape, lambda i, pt: (0,) * wbig.ndim), pl.BlockSpec(pe8.shape, const3),
                  pl.BlockSpec(cmp_w_bf.shape, const3)],
        out_specs=[pl.BlockSpec((None, N_HEADS, KVD), lambda i, pt: (i, 0, 0)),
                   pl.BlockSpec((None, N_KV, LANES), lambda i, pt: (i, 0, 0))],
        scratch_shapes=[pltpu.VMEM((2, n_pages, KVD, PAGE_SIZE), F32),
                        pltpu.VMEM((2, past_len, LANES), F32),
                        pltpu.SemaphoreType.DMA((2,))])
    return pl.pallas_call(
        functools.partial(_cmp_sample_kernel, layer=layer, n_pages=n_pages, past_len=past_len, n_pick=n_pick),
        grid_spec=grid_spec,
        out_shape=[jax.ShapeDtypeStruct((nb, N_HEADS, KVD), F32), jax.ShapeDtypeStruct((nb, N_KV, LANES), jnp.int32)],
        compiler_params=_cparams("arbitrary"),
        name="cmp_sample",
    )(pt_flat, qc_pad, pool, wbig, pe8, cmp_w_bf)


def _sel_sample_kernel(pt_ref, idx_ref, q_ref, pool_ref, win_ref, ks_ref, kw_ref, gate_ref, ocmp_ref, o_ref, buf, sem,
                       *, layer, n_pages, n_pick):
    b = pl.program_id(0)
    slot = b % 2
    nk = n_pick * PAGE_SIZE

    def block_id(bb, g, k):
        return idx_ref[(bb * N_KV + g) * n_pick + k]

    def fetch(bb, sl):
        for g in range(N_KV):
            for k in range(n_pick):
                page = pt_ref[bb * n_pages + block_id(bb, g, k) // (PAGE_SIZE // SEL_BLOCK)]
                pltpu.make_async_copy(pool_ref.at[layer, page], buf.at[sl, g, :, pl.ds(k * PAGE_SIZE, PAGE_SIZE)],
                                      sem.at[sl]).start()

    @pl.when(b == 0)
    def _():
        fetch(0, 0)

    @pl.when(b + 1 < pl.num_programs(0))
    def _():
        fetch(b + 1, 1 - slot)

    for g in range(N_KV):
        for k in range(n_pick):
            pltpu.make_async_copy(pool_ref.at[layer, 0], buf.at[slot, g, :, pl.ds(k * PAGE_SIZE, PAGE_SIZE)],
                                  sem.at[slot]).wait()

    q = q_ref[...]
    lane = _iota((1, nk), 1)
    valid = []
    for g in range(N_KV):
        half = jnp.zeros((1, nk), jnp.int32)
        for k in range(n_pick):
            half = jnp.where(lane // PAGE_SIZE == k, block_id(b, g, k) % (PAGE_SIZE // SEL_BLOCK), half)
        valid.append(jnp.where((lane // SEL_BLOCK) % (PAGE_SIZE // SEL_BLOCK) == half, 0.0, NEG))
    slabs = [buf[slot, g].astype(BF16) for g in range(N_KV)]
    first = _iota((N_HEADS, nk), 0) < GROUP
    s = jnp.where(first, _nn(q, slabs[0]) + valid[0], _nn(q, slabs[1]) + valid[1])
    o_sel = _attend_with_self(q, s, slabs, _iota((N_HEADS, KVD), 0) < GROUP, ks_ref[...])

    win = win_ref[...].astype(BF16)
    o_win = _attend_with_self(q, _nn(q, win), [win], None, kw_ref[...])

    gt = gate_ref[...]
    pick = lambda branch: jnp.sum(jnp.where(_iota((N_HEADS, LANES), 1) == branch * N_HEADS + _iota((N_HEADS, LANES), 0), gt, 0.0),
                                  axis=-1, keepdims=True)
    o = pick(0) * ocmp_ref[...] + pick(1) * o_sel + pick(2) * o_win
    o_ref[...] = jnp.where(_own_value_lanes(), o, 0.0)


def _sel_sample(pt_flat, idx_flat, q_pad, pool, win, ks_new, kw_new, gates, o_cmp, *, layer, n_pages, n_pick):
    nb = q_pad.shape[0]
    wb = win.shape[-1]
    per_b = lambda i, pt, ix: (i, 0, 0)
    grid_spec = pltpu.PrefetchScalarGridSpec(
        num_scalar_prefetch=2, grid=(nb,),
        in_specs=[pl.BlockSpec((None, N_HEADS, KVD), per_b),
                  pl.BlockSpec(memory_space=pl.ANY),
                  pl.BlockSpec((None, None, KVD, wb), lambda i, pt, ix: (layer, i, 0, 0)),
                  pl.BlockSpec((None, 1, KVD), per_b), pl.BlockSpec((None, 1, KVD), per_b),
                  pl.BlockSpec((None, 1, LANES), per_b),
                  pl.BlockSpec((None, N_HEADS, KVD), per_b)],
        out_specs=pl.BlockSpec((None, N_HEADS, KVD), per_b),
        scratch_shapes=[pltpu.VMEM((2, N_KV, KVD, n_pick * PAGE_SIZE), F32), pltpu.SemaphoreType.DMA((2,))])
    return pl.pallas_call(
        functools.partial(_sel_sample_kernel, layer=layer, n_pages=n_pages, n_pick=n_pick),
        grid_spec=grid_spec,
        out_shape=jax.ShapeDtypeStruct((nb, N_HEADS, KVD), F32),
        compiler_params=_cparams("arbitrary"),
        name="sel_sample",
    )(pt_flat, idx_flat, q_pad, pool, win, ks_new, kw_new, gates, o_cmp)


def _swa_sample_kernel(q_ref, win_ref, kv_ref, sink_ref, o_ref):
    q = q_ref[...]
    win = win_ref[...].astype(BF16)
    o = _attend_with_self(q, _nn(q, win), [win], None, kv_ref[...], sink=sink_ref[...])
    o_ref[...] = jnp.where(_own_value_lanes(), o, 0.0)


def _swa_sample(q_pad, win, kv_new, sinks, *, layer):
    nb = q_pad.shape[0]
    wb = win.shape[-1]
    per_b = lambda i: (i, 0, 0)
    return pl.pallas_call(
        _swa_sample_kernel,
        grid=(nb,),
        in_specs=[pl.BlockSpec((None, N_HEADS, KVD), per_b),
                  pl.BlockSpec((None, None, KVD, wb), lambda i: (layer, i, 0, 0)),
                  pl.BlockSpec((None, 1, KVD), per_b),
                  pl.BlockSpec((N_HEADS, 1), lambda i: (0, 0))],
        out_specs=pl.BlockSpec((None, N_HEADS, KVD), per_b),
        out_shape=jax.ShapeDtypeStruct((nb, N_HEADS, KVD), F32),
        compiler_params=_cparams("parallel"),
        name="swa_sample",
    )(q_pad, win, kv_new, sinks.astype(F32).reshape(N_HEADS, 1))


def _pad_queries(q_heads, width):
    q = jnp.transpose(q_heads, (1, 0, 2))
    tiled = jnp.tile(q, (1, 1, width // HEAD_DIM))
    head_g = jnp.arange(N_HEADS)[:, None] // GROUP
    lane_g = jnp.arange(width)[None, :] // HEAD_DIM
    return jnp.where((head_g == lane_g)[None], tiled, jnp.zeros_like(tiled))


def _pool_view(cache):
    l, n_phys = cache.shape[:2]
    return jnp.transpose(cache, (0, 1, 3, 4, 5, 2)).reshape(l, n_phys, KVD, PAGE_SIZE)


def _state_view(state):
    l, nb, w = state.shape[:3]
    return jnp.transpose(state, (0, 1, 3, 4, 5, 2)).reshape(l, nb, KVD, w)


def _pad_w_out(w_out):
    d = w_out.shape[1]
    w = w_out.reshape(N_HEADS, 1, HEAD_DIM, d)
    head_g = jnp.arange(N_HEADS) // GROUP
    slot = jnp.arange(KVD // HEAD_DIM)
    keep = (slot[None, :] == head_g[:, None] + N_KV)[:, :, None, None]
    return jnp.where(keep, w, 0.0).reshape(N_HEADS * KVD, d).astype(BF16)


def _pe8(cmp_pe):
    flat = cmp_pe.reshape(2, 1, CMP_BLOCK * HEAD_DIM)
    return jnp.broadcast_to(flat, (2, 8, CMP_BLOCK * HEAD_DIM)).astype(BF16)


_NO_WBIG = lambda: jnp.zeros((1, 8, LANES), BF16)


def _nsa_prompt_mixer(h, g_pre, w_pad, wbig, pe8, cmp_w_bf, pos, *, tm):
    qn, qr, kvc_t, kvs_t, kvw_t, ks_bf, kw_bf, gates, cp = _project(h, g_pre, w_pad, pos, wbig, nsa=True, tm=tm)
    kcv = _cmpfin(cp, pe8, cmp_w_bf)
    o = _nsa_attention(qn, qr, kcv, ks_bf, kw_bf, gates)
    return o, kvc_t, kvs_t, kvw_t


def _swa_prompt_mixer(h, g_pre, w_pad, sinks, pos, *, tm):
    qr, kv_t, kv_bf = _project(h, g_pre, w_pad, pos, _NO_WBIG(), nsa=False, tm=tm)
    return _swa_attention(qr, kv_bf, sinks), kv_t


def _sample_pick_count(past_len):
    return min(N_SEL, past_len // SEL_BLOCK + 1) - 1


def _token_rows(t):
    return jnp.transpose(t[0], (1, 0))[:, None, :]


def _nsa_sample_mixer(hs, g_pre, w_pad, wbig, pe8, cmp_w_bf, pt_flat, pool_c, pool_s, win_a, *, layer, n_pages):
    nb = hs.shape[0]
    past_len = n_pages * PAGE_SIZE
    pos = jnp.full((nb,), past_len, jnp.int32)
    qn, qr, kvc_t, kvs_t, kvw_t, _, _, gates, _ = _project(hs[None], g_pre, w_pad, pos, wbig, nsa=True, tm=nb)
    n_pick = _sample_pick_count(past_len)
    o_cmp, idx = _cmp_sample(pt_flat, _pad_queries(qn[0], LANES), pool_c, wbig, pe8, cmp_w_bf,
                             layer=layer, n_pages=n_pages, n_pick=n_pick)
    o = _sel_sample(pt_flat, idx[:, :, :n_pick].reshape(-1), _pad_queries(qr[0], KVD), pool_s, win_a,
                    _token_rows(kvs_t), _token_rows(kvw_t), jnp.transpose(gates, (1, 0, 2)), o_cmp,
                    layer=layer, n_pages=n_pages, n_pick=n_pick)
    return o.reshape(nb, N_HEADS * KVD), kvc_t[0], kvs_t[0], kvw_t[0]


def _swa_sample_mixer(hs, g_pre, w_pad, sinks, win_b, *, layer, past_len):
    nb = hs.shape[0]
    pos = jnp.full((nb,), past_len, jnp.int32)
    qr, kv_t, _ = _project(hs[None], g_pre, w_pad, pos, _NO_WBIG(), nsa=False, tm=nb)
    o = _swa_sample(_pad_queries(qr[0], KVD), win_b, _token_rows(kv_t), sinks, layer=layer)
    return o.reshape(nb, N_HEADS * KVD), kv_t[0]


def _pad_w_in(w):
    n = w.shape[1]
    n_pad = -(-n // LANES) * LANES
    return jnp.pad(w, ((0, 0), (0, n_pad - n))).astype(BF16)


def _wbig(cmp_w):
    cw = cmp_w.reshape(2, CMP_RATIO, CMP_STRIDE, HEAD_DIM, HEAD_DIM)
    eye = jnp.eye(N_KV, dtype=cmp_w.dtype)
    big = jnp.einsum('crsde,gy->csydrge', cw, eye)
    return big.reshape(2, CMP_STRIDE, LANES, CMP_RATIO * LANES).astype(BF16)


PROJ_TILE = 512
POST_TILE = 256


def _rows_from_slab(t):
    lead = t.shape[:-2]
    n = t.shape[-1]
    t = t.reshape(lead + (2, N_KV, HEAD_DIM, n))
    k = len(lead)
    return jnp.transpose(t, tuple(range(k)) + (k + 3, k, k + 1, k + 2))


def kernel(x_prompt, x_sample, page_table, cache_a_cmp, cache_a_sel, state_a_win, state_b_win, norms, ffn_w_gu, ffn_w_down, a_w_in, a_w_out, a_cmp_w, a_cmp_pe, b_w_in, b_w_out, b_sinks):
    b, s, d = x_prompt.shape
    nb, t, _ = x_sample.shape
    assert t == 1
    n_pages = page_table.shape[1]
    past_len = n_pages * PAGE_SIZE
    depth = norms.shape[0]
    pos = jnp.arange(s, dtype=jnp.int32)
    pt_flat = page_table.reshape(-1).astype(jnp.int32)
    pool_c, pool_s = _pool_view(cache_a_cmp), _pool_view(cache_a_sel)
    win_a, win_b = _state_view(state_a_win), _state_view(state_b_win)
    wa, wbw = win_a.shape[-1], win_b.shape[-1]

    hp, hs = x_prompt, x_sample.reshape(nb, d)
    outs = {k: [] for k in ("cmp_p", "cmp_s", "sel_p", "sel_s", "awin_p", "awin_s", "bwin_p", "bwin_s")}
    for i in range(depth):
        j = i // 2
        g_pre, g_post, f_pre, f_post = norms[i, 0], norms[i, 1], norms[i, 2], norms[i, 3]
        w_gu, w_down = ffn_w_gu[i].astype(BF16), ffn_w_down[i].astype(BF16)
        if i % 2 == 0:
            w_pad, wbig, pe8, cw = _pad_w_in(a_w_in[j]), _wbig(a_cmp_w[j]), _pe8(a_cmp_pe[j]), a_cmp_w[j].astype(BF16)
            op, c_p, s_p, w_p = _nsa_prompt_mixer(hp, g_pre, w_pad, wbig, pe8, cw, pos, tm=PROJ_TILE)
            os_, c_s, s_s, w_s = _nsa_sample_mixer(hs, g_pre, w_pad, wbig, pe8, cw, pt_flat, pool_c, pool_s, win_a,
                                                   layer=j, n_pages=n_pages)
            w_out = a_w_out[j]
            outs["cmp_p"].append(c_p)
            outs["cmp_s"].append(c_s)
            outs["sel_p"].append(s_p)
            outs["sel_s"].append(s_s)
            outs["awin_p"].append(w_p[:, :, s - min(A_WINDOW, s):])
            outs["awin_s"].append(jnp.concatenate([win_a[j], w_s[None].transpose(2, 1, 0)], axis=-1)[:, :, t:])
        else:
            w_pad = _pad_w_in(b_w_in[j])
            op, w_p = _swa_prompt_mixer(hp, g_pre, w_pad, b_sinks[j], pos, tm=PROJ_TILE)
            os_, w_s = _swa_sample_mixer(hs, g_pre, w_pad, b_sinks[j], win_b, layer=j, past_len=past_len)
            w_out = b_w_out[j]
            outs["bwin_p"].append(w_p[:, :, s - min(B_WINDOW, s):])
            outs["bwin_s"].append(jnp.concatenate([win_b[j], w_s[None].transpose(2, 1, 0)], axis=-1)[:, :, t:])
        hp = _post(hp.reshape(b * s, d), op.reshape(b * s, -1), w_out.astype(BF16), g_post, f_pre, f_post,
                   w_gu, w_down, tm=POST_TILE).reshape(b, s, d)
        hs = _post(hs, os_, _pad_w_out(w_out), g_post, f_pre, f_post, w_gu, w_down, tm=nb)

    stack = lambda key: _rows_from_slab(jnp.stack(outs[key]))
    new_tok = lambda key: jnp.transpose(jnp.stack(outs[key]).reshape(-1, 2, N_KV, HEAD_DIM, nb), (0, 4, 1, 2, 3))[:, :, None]
    return (hp, hs.reshape(nb, 1, d), stack("cmp_p"), new_tok("cmp_s"), stack("sel_p"), new_tok("sel_s"),
            stack("awin_p"), stack("awin_s"), stack("bwin_p"), stack("bwin_s"))
```

```python
import functools

import jax
import jax.numpy as jnp
from jax import lax
from jax.experimental import pallas as pl
from jax.experimental.pallas import tpu as pltpu

F32 = jnp.float32
BF16 = jnp.bfloat16

HEAD_DIM = 64
ROT_DIM = HEAD_DIM // 4
ROPE_THETA = 500000.0
RMS_EPS = 1e-6
N_HEADS = 16
N_KV = 2
GROUP = N_HEADS // N_KV
KVD = 2 * N_KV * HEAD_DIM
CMP_BLOCK = 32
CMP_STRIDE = 16
CMP_RATIO = CMP_BLOCK // CMP_STRIDE
SEL_BLOCK = 64
N_SEL = 16
A_WINDOW = 512
B_WINDOW = 128
FORCE_BONUS = 1000.0
PAGE_SIZE = 128
ATTN_SCALE = HEAD_DIM ** -0.5
LANES = 128
Q_TILE = 256
SEL_KEY_TILE = 512
NEG = -1e30
PICKED = -3e30
VMEM_LIMIT = 56 * 1024 * 1024


def _cparams(*sem):
    return pltpu.CompilerParams(dimension_semantics=sem, vmem_limit_bytes=VMEM_LIMIT)


def _rms(x, g):
    return x * lax.rsqrt(jnp.mean(x * x, axis=-1, keepdims=True) + RMS_EPS) * g


def _nt(a, b):
    return lax.dot_general(a, b, (((1,), (1,)), ((), ())), preferred_element_type=F32)


def _nn(a, b):
    return jnp.dot(a, b, preferred_element_type=F32)


def _split3(x):
    hi = x.astype(BF16)
    r1 = x - hi.astype(F32)
    mid = r1.astype(BF16)
    lo = (r1 - mid.astype(F32)).astype(BF16)
    return hi, mid, lo


def _rope_tables(pos):
    half = ROT_DIM // 2
    inv = ROPE_THETA ** (-jnp.arange(half, dtype=F32) * (2.0 / ROT_DIM))
    ang = pos.astype(F32)[:, None] * inv[None, :]
    cos, sin = jnp.cos(ang), jnp.sin(ang)
    t = pos.shape[0]
    c64 = jnp.concatenate([cos, cos, jnp.ones((t, HEAD_DIM - ROT_DIM), F32)], axis=-1)
    up64 = jnp.concatenate([jnp.zeros((t, half), F32), sin, jnp.zeros((t, HEAD_DIM - ROT_DIM), F32)], axis=-1)
    dn64 = jnp.concatenate([-sin, jnp.zeros((t, HEAD_DIM - half), F32)], axis=-1)
    two = lambda a: jnp.concatenate([a, a], axis=-1)
    return two(c64), two(up64), two(dn64)


CHUNK_PITCH = 24


def _chunk_products(rows_ref, wbig_ref, pitch=CMP_STRIDE):
    n_chunks = rows_ref.shape[1] // pitch
    acc = []
    for c in range(2):
        flat = [rows_ref[c, pl.ds(s, n_chunks, stride=pitch), :].astype(BF16) for s in range(CMP_STRIDE)]
        acc.append(_nn(jnp.concatenate(flat, axis=1), wbig_ref[c]))
    return jnp.concatenate([acc[c][:, r * LANES:(r + 1) * LANES] for r in range(CMP_RATIO) for c in range(2)], axis=-1)


def _proj_kernel(x_ref, g_ref, w_ref, c_ref, up_ref, dn_ref, wbig_ref, *refs, nsa):
    if nsa:
        (qn_ref, qr_ref, kvc_t_ref, kvs_t_ref, kvw_t_ref, ks_bf_ref, kw_bf_ref, gate_ref, cp_ref, kvc_scr) = refs
    else:
        (qr_ref, kv_t_ref, kv_bf_ref) = refs
    half = ROT_DIM // 2
    y = _rms(x_ref[...], g_ref[...])
    p = _nn(y.astype(BF16), w_ref[...])
    c, up, dn = c_ref[...], up_ref[...], dn_ref[...]

    def rope(v):
        return v * c + pltpu.roll(v, half, 1) * up + pltpu.roll(v, LANES - half, 1) * dn

    for j in range(N_HEADS // 2):
        ch = p[:, j * LANES:(j + 1) * LANES]
        rot = (rope(ch) * ATTN_SCALE).astype(BF16)
        for t in range(2):
            qr_ref[2 * j + t] = rot[:, t * HEAD_DIM:(t + 1) * HEAD_DIM]
        if nsa:
            raw = (ch * ATTN_SCALE).astype(BF16)
            for t in range(2):
                qn_ref[2 * j + t] = raw[:, t * HEAD_DIM:(t + 1) * HEAD_DIM]
    base = N_HEADS * HEAD_DIM

    def roped_kv(off):
        k = rope(p[:, off:off + LANES])
        v = p[:, off + LANES:off + KVD]
        return jnp.concatenate([k, v], axis=-1).T

    if nsa:
        kvc = p[:, base:base + KVD]
        kvc_t_ref[...] = kvc.T
        kvs_t = roped_kv(base + KVD)
        kvw_t = roped_kv(base + 2 * KVD)
        kvs_t_ref[...] = kvs_t
        kvw_t_ref[...] = kvw_t
        ks_bf_ref[...] = kvs_t.astype(BF16)
        kw_bf_ref[...] = kvw_t.astype(BF16)
        gate_ref[...] = jax.nn.sigmoid(p[:, base + 3 * KVD:base + 3 * KVD + LANES])
        for c in range(2):
            kvc_scr[c] = kvc[:, c * LANES:(c + 1) * LANES]
        cp_ref[...] = _chunk_products(kvc_scr, wbig_ref)
    else:
        kv_t = roped_kv(base)
        kv_t_ref[...] = kv_t
        kv_bf_ref[...] = kv_t.astype(BF16)


def _project(h, g_pre, w_pad, pos, wbig, *, nsa, tm):
    b, s, d = h.shape
    n = w_pad.shape[1]
    c, up, dn = _rope_tables(pos)
    grid = (b, s // tm)
    tok = lambda bb, i: (bb, i, 0)
    feat = lambda bb, i: (bb, 0, i)
    head = lambda bb, i: (bb, 0, i, 0)
    const2 = lambda bb, i: (0, 0)
    const3 = lambda bb, i: (0, 0, 0)
    tab = lambda bb, i: (i, 0)
    in_specs = [
        pl.BlockSpec((None, tm, d), tok),
        pl.BlockSpec((1, d), const2),
        pl.BlockSpec((d, n), const2),
        pl.BlockSpec((tm, LANES), tab),
        pl.BlockSpec((tm, LANES), tab),
        pl.BlockSpec((tm, LANES), tab),
        pl.BlockSpec(wbig.shape, lambda bb, i: (0,) * wbig.ndim),
    ]
    q_spec = pl.BlockSpec((None, N_HEADS, tm, HEAD_DIM), head)
    q_shape = jax.ShapeDtypeStruct((b, N_HEADS, s, HEAD_DIM), BF16)
    t_spec = pl.BlockSpec((None, KVD, tm), feat)
    t_f32 = jax.ShapeDtypeStruct((b, KVD, s), F32)
    t_bf = jax.ShapeDtypeStruct((b, KVD, s), BF16)
    if nsa:
        out_specs = [q_spec, q_spec, t_spec, t_spec, t_spec, t_spec, t_spec,
                     pl.BlockSpec((None, tm, LANES), tok),
                     pl.BlockSpec((None, tm // CMP_STRIDE, CMP_RATIO * KVD), tok)]
        out_shape = [q_shape, q_shape, t_f32, t_f32, t_f32, t_bf, t_bf,
                     jax.ShapeDtypeStruct((b, s, LANES), F32),
                     jax.ShapeDtypeStruct((b, s // CMP_STRIDE, CMP_RATIO * KVD), F32)]
        scratch = [pltpu.VMEM((2, tm, LANES), F32)]
    else:
        out_specs = [q_spec, t_spec, t_spec]
        out_shape = [q_shape, t_f32, t_bf]
        scratch = []
    return pl.pallas_call(
        functools.partial(_proj_kernel, nsa=nsa),
        grid=grid, in_specs=in_specs, out_specs=out_specs, out_shape=out_shape,
        scratch_shapes=scratch, compiler_params=_cparams("parallel", "parallel"),
        name="proj_nsa" if nsa else "proj_swa",
    )(h, g_pre.reshape(1, d), w_pad, c, up, dn, wbig)


def _post_kernel(h_ref, o_ref, wout_ref, gpost_ref, fpre_ref, fpost_ref, wgu_ref, wdown_ref, out_ref):
    d_ff = wdown_ref.shape[0]
    a = _nn(o_ref[...].astype(BF16), wout_ref[...])
    h = h_ref[...] + _rms(a, gpost_ref[...])
    x = _rms(h, fpre_ref[...]).astype(BF16)
    gu = _nn(x, wgu_ref[...])
    g, u = gu[:, :d_ff], gu[:, d_ff:]
    act = (g * jax.nn.sigmoid(g) * u).astype(BF16)
    out_ref[...] = h + _rms(_nn(act, wdown_ref[...]), fpost_ref[...])


def _post(h, o, w_out, g_post, f_pre, f_post, w_gu, w_down, *, tm):
    t, d = h.shape
    kd = o.shape[1]
    d_ff = w_down.shape[0]
    row = lambda i: (i, 0)
    const = lambda i: (0, 0)
    once = pl.Buffered(1)
    vec = lambda: pl.BlockSpec((1, d), const)
    return pl.pallas_call(
        _post_kernel,
        grid=(t // tm,),
        in_specs=[pl.BlockSpec((tm, d), row), pl.BlockSpec((tm, kd), row),
                  pl.BlockSpec((kd, d), const, pipeline_mode=once), vec(), vec(), vec(),
                  pl.BlockSpec((d, 2 * d_ff), const, pipeline_mode=once),
                  pl.BlockSpec((d_ff, d), const, pipeline_mode=once)],
        out_specs=pl.BlockSpec((tm, d), row),
        out_shape=jax.ShapeDtypeStruct((t, d), F32),
        compiler_params=_cparams("parallel"),
        name="post_ffn",
    )(h, o, w_out, g_post.reshape(1, d), f_pre.reshape(1, d), f_post.reshape(1, d), w_gu, w_down)


def _iota(shape, axis):
    return lax.broadcasted_iota(jnp.int32, shape, axis)


def _cmp_finish(cp, pe_ref, w_ref):
    ncp = cp.shape[0]
    out = []
    for c in range(2):
        pew = _nn(pe_ref[c], w_ref[c])[0:1]
        first = cp[:, c * LANES:(c + 1) * LANES]
        second = pltpu.roll(cp[:, KVD + c * LANES:KVD + (c + 1) * LANES], ncp - 1, 0)
        out.append(first + second + jnp.concatenate([pew, pew], axis=-1))
    return out


def _cmpfin_kernel(cp_ref, pe_ref, w_ref, out_ref):
    for c, both in enumerate(_cmp_finish(cp_ref[...], pe_ref, w_ref)):
        for g in range(N_KV):
            out_ref[c, g] = both[:, g * HEAD_DIM:(g + 1) * HEAD_DIM].astype(BF16)


def _cmpfin(cp, pe8, cmp_w_bf):
    b, ncp, _ = cp.shape
    return pl.pallas_call(
        _cmpfin_kernel,
        grid=(b,),
        in_specs=[pl.BlockSpec((None, ncp, CMP_RATIO * KVD), lambda i: (i, 0, 0)),
                  pl.BlockSpec(pe8.shape, lambda i: (0, 0, 0)),
                  pl.BlockSpec(cmp_w_bf.shape, lambda i: (0, 0, 0))],
        out_specs=pl.BlockSpec((None, 2, N_KV, ncp, HEAD_DIM), lambda i: (i, 0, 0, 0, 0)),
        out_shape=jax.ShapeDtypeStruct((b, 2, N_KV, ncp, HEAD_DIM), BF16),
        compiler_params=_cparams("parallel"),
        name="cmp_finish",
    )(cp, pe8, cmp_w_bf)


ONES_ROWS = 16
ACC_W = HEAD_DIM + ONES_ROWS


def _with_ones(v_t):
    return jnp.concatenate([v_t, jnp.ones((ONES_ROWS, v_t.shape[1]), BF16)], axis=0)


def _online_tile(q_aug, k_aug, v_aug, carry, mask=None):
    m, acc = carry
    tk = k_aug.shape[1]
    s = _nn(q_aug, k_aug).reshape(GROUP, Q_TILE, tk)
    if mask is not None:
        s = jnp.where(mask[None], s, NEG)
    m_new = jnp.maximum(m, jnp.max(s, axis=-1, keepdims=True))
    p = jnp.exp(s - m_new).reshape(GROUP * Q_TILE, tk).astype(BF16)
    return m_new, jnp.exp(m - m_new) * acc + _nt(p, v_aug).reshape(GROUP, Q_TILE, ACC_W)


def _window_pass(q, k_ref, v_ref, i, window):
    start = i * Q_TILE
    width = window + Q_TILE
    k0 = pl.multiple_of(jnp.maximum(start - window, 0), LANES)
    dist = (start + _iota((Q_TILE, width), 0)) - (k0 + _iota((Q_TILE, width), 1))
    s = _nn(q, k_ref[:, pl.ds(k0, width)]).reshape(GROUP, Q_TILE, width)
    s = jnp.where(((dist >= 0) & (dist <= window))[None], s, NEG)
    m = jnp.max(s, axis=-1, keepdims=True)
    p = jnp.exp(s - m).reshape(GROUP * Q_TILE, width).astype(BF16)
    return m, _nt(p, _with_ones(v_ref[:, pl.ds(k0, width)])).reshape(GROUP, Q_TILE, ACC_W)


def _nsa_attn_kernel(qn_ref, qr_ref, kc_ref, vc_ref, ks_ref, vs_ref, kw_ref, vw_ref, gate_ref, blk_ref, o_ref, *, n_pick):
    g = pl.program_id(1)
    i = pl.program_id(2)
    start = i * Q_TILE
    rows = GROUP * Q_TILE
    ncp = kc_ref.shape[0]
    qn = qn_ref[...].reshape(rows, HEAD_DIM)
    qr = qr_ref[...].reshape(rows, HEAD_DIM)

    s = _nt(qn, kc_ref[...]).reshape(GROUP, Q_TILE, ncp)
    cend = _iota((Q_TILE, ncp), 1) * CMP_STRIDE + (CMP_BLOCK - 1)
    cmask = (cend <= start + _iota((Q_TILE, ncp), 0))[None]
    s = jnp.where(cmask, s, NEG)
    m = jnp.max(s, axis=-1, keepdims=True)
    e = jnp.exp(s - m)
    p = e * jnp.where(m > 0.5 * NEG, 1.0 / jnp.sum(e, axis=-1, keepdims=True), 0.0)
    o_cmp = _nn(p.reshape(rows, ncp).astype(BF16), vc_ref[...]).reshape(GROUP, Q_TILE, HEAD_DIM)

    c0 = _iota((ncp, LANES), 0) * CMP_STRIDE
    s0 = _iota((ncp, LANES), 1) * SEL_BLOCK
    overlap = jnp.where((c0 < s0 + SEL_BLOCK) & (c0 + CMP_BLOCK > s0), 1.0, 0.0).astype(BF16)
    imp = sum(_nn(t, overlap) for t in _split3(jnp.sum(p, axis=0)))

    _, acc_win = _window_pass(qr, kw_ref, vw_ref, i, A_WINDOW)

    lane = _iota((Q_TILE, LANES), 1)
    lane_f = lane.astype(F32)
    qpos = start + _iota((Q_TILE, LANES), 0)
    qblk = qpos // SEL_BLOCK
    forced = (lane == 0) | (lane == qblk) | (lane == qblk - 1)
    allowed = lane * SEL_BLOCK <= qpos
    score = jnp.where(allowed, imp + jnp.where(forced, FORCE_BONUS, 0.0), NEG)
    off = jnp.full((Q_TILE, LANES), NEG, F32)
    for _ in range(n_pick):
        best = jnp.max(score, axis=-1, keepdims=True)
        idx = jnp.min(jnp.where(score == best, lane_f, float(LANES)), axis=-1, keepdims=True)
        pick = lane_f == idx
        off = jnp.where(pick & allowed, 0.0, off)
        score = jnp.where(pick, PICKED, score)

    tk = SEL_KEY_TILE
    off_bf = off.astype(BF16)
    q_aug = jnp.concatenate([jnp.concatenate([off_bf] * GROUP, axis=0), qr], axis=1)

    def sel_tile(kt, carry, mask=None):
        k0 = pl.multiple_of(kt * tk, tk)
        k_aug = jnp.concatenate([blk_ref[:, pl.ds(k0, tk)], ks_ref[:, pl.ds(k0, tk)]], axis=0)
        return _online_tile(q_aug, k_aug, _with_ones(vs_ref[:, pl.ds(k0, tk)]), carry, mask)

    last = start // tk
    init = (jnp.full((GROUP, Q_TILE, 1), NEG, F32), jnp.zeros((GROUP, Q_TILE, ACC_W), F32))
    carry = lax.fori_loop(0, last, sel_tile, init)
    causal = (last * tk + _iota((Q_TILE, tk), 1)) <= (start + _iota((Q_TILE, tk), 0))
    _, acc_sel = sel_tile(last, carry, causal)

    gt = gate_ref[...]

    def gate(branch, head):
        col = branch * N_HEADS + head
        return jnp.where(g == 0, gt[:, col:col + 1], gt[:, col + GROUP:col + GROUP + 1])

    def scaled(branch, head, acc):
        return (gate(branch, head) / acc[head, :, HEAD_DIM:HEAD_DIM + 1]) * acc[head, :, :HEAD_DIM]

    heads = [gate(0, h) * o_cmp[h] + scaled(1, h, acc_sel) + scaled(2, h, acc_win) for h in range(GROUP)]
    o_ref[...] = jnp.concatenate(heads, axis=-1).astype(o_ref.dtype)


def _nsa_attention(qn, qr, kcv, ks_bf, kw_bf, gates):
    b, _, s, _ = qn.shape
    ncp = kcv.shape[3]
    assert s % SEL_KEY_TILE == 0 and s // SEL_BLOCK <= LANES and s >= A_WINDOW + Q_TILE
    blk_rows = (jnp.arange(LANES)[:, None] == (jnp.arange(s) // SEL_BLOCK)[None, :]).astype(BF16)
    q_spec = pl.BlockSpec((None, GROUP, Q_TILE, HEAD_DIM), lambda bb, g, i: (bb, g, i, 0))
    cmp_spec = lambda c: pl.BlockSpec((None, None, None, ncp, HEAD_DIM), lambda bb, g, i: (bb, c, g, 0, 0))
    slab = lambda c: pl.BlockSpec((None, HEAD_DIM, s), lambda bb, g, i: (bb, c * N_KV + g, 0))
    return pl.pallas_call(
        functools.partial(_nsa_attn_kernel, n_pick=min(N_SEL, s // SEL_BLOCK)),
        grid=(b, N_KV, s // Q_TILE),
        in_specs=[q_spec, q_spec, cmp_spec(0), cmp_spec(1), slab(0), slab(1), slab(0), slab(1),
                  pl.BlockSpec((None, Q_TILE, LANES), lambda bb, g, i: (bb, i, 0)),
                  pl.BlockSpec((LANES, s), lambda bb, g, i: (0, 0), pipeline_mode=pl.Buffered(1))],
        out_specs=pl.BlockSpec((None, Q_TILE, GROUP * HEAD_DIM), lambda bb, g, i: (bb, i, g)),
        out_shape=jax.ShapeDtypeStruct((b, s, N_HEADS * HEAD_DIM), BF16),
        compiler_params=_cparams("parallel", "parallel", "arbitrary"),
        name="nsa_attn",
    )(qn, qr, kcv, kcv, ks_bf, ks_bf, kw_bf, kw_bf, gates, blk_rows)


def _swa_attn_kernel(qr_ref, k_ref, v_ref, sink_ref, o_ref):
    i = pl.program_id(2)
    q = qr_ref[...].reshape(GROUP * Q_TILE, HEAD_DIM)
    m, acc = _window_pass(q, k_ref, v_ref, i, B_WINDOW)
    sink = sink_ref[...]
    m_f = jnp.maximum(m, sink)
    scale = jnp.exp(m - m_f)
    out = acc[:, :, :HEAD_DIM] * scale / (acc[:, :, HEAD_DIM:HEAD_DIM + 1] * scale + jnp.exp(sink - m_f))
    o_ref[...] = jnp.concatenate([out[h] for h in range(GROUP)], axis=-1).astype(o_ref.dtype)


def _swa_attention(qr, kv_bf, sinks):
    b, _, s, _ = qr.shape
    assert s >= B_WINDOW + Q_TILE
    sink = jnp.broadcast_to(sinks.astype(F32).reshape(N_KV, GROUP, 1, 1), (N_KV, GROUP, Q_TILE, 1))
    slab = lambda c: pl.BlockSpec((None, HEAD_DIM, s), lambda bb, g, i: (bb, c * N_KV + g, 0))
    return pl.pallas_call(
        _swa_attn_kernel,
        grid=(b, N_KV, s // Q_TILE),
        in_specs=[pl.BlockSpec((None, GROUP, Q_TILE, HEAD_DIM), lambda bb, g, i: (bb, g, i, 0)),
                  slab(0), slab(1),
                  pl.BlockSpec((None, GROUP, Q_TILE, 1), lambda bb, g, i: (g, 0, 0, 0))],
        out_specs=pl.BlockSpec((None, Q_TILE, GROUP * HEAD_DIM), lambda bb, g, i: (bb, i, g)),
        out_shape=jax.ShapeDtypeStruct((b, s, N_HEADS * HEAD_DIM), BF16),
        compiler_params=_cparams("parallel", "parallel", "arbitrary"),
        name="swa_attn",
    )(qr, kv_bf, kv_bf, sink)


def _own_value_lanes():
    row_g = _iota((N_HEADS, KVD), 0) // GROUP
    lane = _iota((N_HEADS, KVD), 1)
    return (lane >= KVD // 2) & ((lane - KVD // 2) // HEAD_DIM == row_g)


def _attend_with_self(q, s, k_t_bf_list, row_sel, self_row, sink=None):
    s_self = jnp.sum(q.astype(F32) * self_row, axis=-1, keepdims=True)
    m = jnp.maximum(jnp.max(s, axis=-1, keepdims=True), s_self)
    if sink is not None:
        m = jnp.maximum(m, sink)
    p = jnp.exp(s - m)
    p_self = jnp.exp(s_self - m)
    den = jnp.sum(p, axis=-1, keepdims=True) + p_self
    if sink is not None:
        den = den + jnp.exp(sink - m)
    p_bf = p.astype(BF16)
    pv = _nt(p_bf, k_t_bf_list[0])
    if len(k_t_bf_list) > 1:
        pv = jnp.where(row_sel, pv, _nt(p_bf, k_t_bf_list[1]))
    return (pv + p_self * self_row) / den


def _cmp_sample_kernel(pt_ref, qc_ref, pool_ref, wbig_ref, pe_ref, w_ref, o_ref, imp_ref, buf, xs, sem,
                       *, layer, n_pages, past_len):
    b = pl.program_id(0)
    slot = b % 2

    def fetch(bb, sl):
        def body(p, carry):
            page = pt_ref[bb * n_pages + p]
            pltpu.make_async_copy(pool_ref.at[layer, page], buf.at[sl, p], sem.at[sl]).start()
            return carry
        lax.fori_loop(0, n_pages, body, 0)

    @pl.when(b == 0)
    def _():
        fetch(0, 0)

    @pl.when(b + 1 < pl.num_programs(0))
    def _():
        fetch(b + 1, 1 - slot)

    def wait(p, carry):
        pltpu.make_async_copy(pool_ref.at[layer, 0], buf.at[slot, p], sem.at[slot]).wait()
        return carry
    lax.fori_loop(0, n_pages, wait, 0)

    per_page = PAGE_SIZE // CMP_STRIDE

    def flip(p, carry):
        r0 = pl.multiple_of(p * (per_page * CHUNK_PITCH), 8)
        for c in range(2):
            rows = buf[slot, p, c * LANES:(c + 1) * LANES, :].T
            for j in range(per_page):
                xs[c, pl.ds(r0 + j * CHUNK_PITCH, CMP_STRIDE), :] = rows[j * CMP_STRIDE:(j + 1) * CMP_STRIDE]
        return carry
    lax.fori_loop(0, n_pages, flip, 0, unroll=4)
    n_chunks = n_pages * per_page
    kc, vc = _cmp_finish(_chunk_products(xs, wbig_ref, CHUNK_PITCH), pe_ref, w_ref)

    q = qc_ref[...]
    s = _nt(q, kc.astype(BF16))
    cmask = _iota((N_HEADS, n_chunks), 1) * CMP_STRIDE + (CMP_BLOCK - 1) <= past_len
    s = jnp.where(cmask, s, NEG)
    m = jnp.max(s, axis=-1, keepdims=True)
    e = jnp.where(cmask, jnp.exp(s - m), 0.0)
    den = jnp.sum(e, axis=-1, keepdims=True)
    p = e / jnp.where(den > 0, den, 1.0)
    o = _nn(p.astype(BF16), vc.astype(BF16))
    own = _iota((N_HEADS, LANES), 0) // GROUP == _iota((N_HEADS, LANES), 1) // HEAD_DIM
    o_ref[...] = jnp.concatenate([jnp.zeros((N_HEADS, LANES), F32), jnp.where(own, o, 0.0)], axis=-1)

    c0 = _iota((n_chunks, LANES), 0) * CMP_STRIDE
    s0 = _iota((n_chunks, LANES), 1) * SEL_BLOCK
    overlap = jnp.where((c0 < s0 + SEL_BLOCK) & (c0 + CMP_BLOCK > s0), 1.0, 0.0).astype(BF16)
    imp_h = sum(_nn(t, overlap) for t in _split3(p))
    imp_ref[...] = jnp.concatenate(
        [jnp.sum(imp_h[g * GROUP:(g + 1) * GROUP], axis=0, keepdims=True) for g in range(N_KV)], axis=0)


def _topk_sample_kernel(imp_ref, idx_ref, *, past_len, n_pick):
    shape = imp_ref.shape
    lane = _iota(shape, 1)
    lane_f = lane.astype(F32)
    qblk = past_len // SEL_BLOCK
    forced = (lane == 0) | (lane == qblk) | (lane == qblk - 1)
    score = jnp.where(lane * SEL_BLOCK < past_len, imp_ref[...] + jnp.where(forced, FORCE_BONUS, 0.0), NEG)
    picks = jnp.zeros(shape, jnp.int32)
    for it in range(n_pick):
        best = jnp.max(score, axis=-1, keepdims=True)
        idx = jnp.min(jnp.where(score == best, lane_f, float(LANES)), axis=-1, keepdims=True)
        score = jnp.where(lane_f == idx, PICKED, score)
        picks = jnp.where(lane == it, idx.astype(jnp.int32), picks)
    idx_ref[...] = picks


def _topk_sample(imp, *, past_len, n_pick):
    rows = imp.shape[0]
    return pl.pallas_call(
        functools.partial(_topk_sample_kernel, past_len=past_len, n_pick=n_pick),
        grid=(1,),
        in_specs=[pl.BlockSpec((rows, LANES), lambda i: (0, 0))],
        out_specs=pl.BlockSpec((rows, LANES), lambda i: (0, 0)),
        out_shape=jax.ShapeDtypeStruct((rows, LANES), jnp.int32),
        compiler_params=_cparams("arbitrary"),
        name="topk_sample",
    )(imp)


def _cmp_sample(pt_flat, qc_pad, pool, wbig, pe8, cmp_w_bf, *, layer, n_pages):
    nb = qc_pad.shape[0]
    past_len = n_pages * PAGE_SIZE
    n_chunks = past_len // CMP_STRIDE
    assert past_len // SEL_BLOCK <= LANES
    const3 = lambda i, pt: (0, 0, 0)
    grid_spec = pltpu.PrefetchScalarGridSpec(
        num_scalar_prefetch=1, grid=(nb,),
        in_specs=[pl.BlockSpec((None, N_HEADS, LANES), lambda i, pt: (i, 0, 0)),
                  pl.BlockSpec(memory_space=pl.ANY),
                  pl.BlockSpec(wbig.shape, lambda i, pt: (0,) * wbig.ndim), pl.BlockSpec(pe8.shape, const3),
                  pl.BlockSpec(cmp_w_bf.shape, const3)],
        out_specs=[pl.BlockSpec((None, N_HEADS, KVD), lambda i, pt: (i, 0, 0)),
                   pl.BlockSpec((None, N_KV, LANES), lambda i, pt: (i, 0, 0))],
        scratch_shapes=[pltpu.VMEM((2, n_pages, KVD, PAGE_SIZE), F32),
                        pltpu.VMEM((2, n_chunks * CHUNK_PITCH, LANES), F32),
                        pltpu.SemaphoreType.DMA((2,))])
    return pl.pallas_call(
        functools.partial(_cmp_sample_kernel, layer=layer, n_pages=n_pages, past_len=past_len),
        grid_spec=grid_spec,
        out_shape=[jax.ShapeDtypeStruct((nb, N_HEADS, KVD), F32), jax.ShapeDtypeStruct((nb, N_KV, LANES), F32)],
        compiler_params=_cparams("arbitrary"),
        name="cmp_sample",
    )(pt_flat, qc_pad, pool, wbig, pe8, cmp_w_bf)


def _sel_sample_kernel(pt_ref, idx_ref, q_ref, pool_ref, win_ref, ks_ref, kw_ref, gate_ref, ocmp_ref, o_ref, buf, sem,
                       *, layer, n_pages, n_pick):
    b = pl.program_id(0)
    slot = b % 2
    nk = n_pick * PAGE_SIZE

    def block_id(bb, g, k):
        return idx_ref[(bb * N_KV + g) * n_pick + k]

    def fetch(bb, sl):
        for g in range(N_KV):
            for k in range(n_pick):
                page = pt_ref[bb * n_pages + block_id(bb, g, k) // (PAGE_SIZE // SEL_BLOCK)]
                pltpu.make_async_copy(pool_ref.at[layer, page], buf.at[sl, g, :, pl.ds(k * PAGE_SIZE, PAGE_SIZE)],
                                      sem.at[sl]).start()

    @pl.when(b == 0)
    def _():
        fetch(0, 0)

    @pl.when(b + 1 < pl.num_programs(0))
    def _():
        fetch(b + 1, 1 - slot)

    for g in range(N_KV):
        for k in range(n_pick):
            pltpu.make_async_copy(pool_ref.at[layer, 0], buf.at[slot, g, :, pl.ds(k * PAGE_SIZE, PAGE_SIZE)],
                                  sem.at[slot]).wait()

    q = q_ref[...]
    lane = _iota((1, nk), 1)
    valid = []
    for g in range(N_KV):
        half = jnp.zeros((1, nk), jnp.int32)
        for k in range(n_pick):
            half = jnp.where(lane // PAGE_SIZE == k, block_id(b, g, k) % (PAGE_SIZE // SEL_BLOCK), half)
        valid.append(jnp.where((lane // SEL_BLOCK) % (PAGE_SIZE // SEL_BLOCK) == half, 0.0, NEG))
    slabs = [buf[slot, g].astype(BF16) for g in range(N_KV)]
    first = _iota((N_HEADS, nk), 0) < GROUP
    s = jnp.where(first, _nn(q, slabs[0]) + valid[0], _nn(q, slabs[1]) + valid[1])
    o_sel = _attend_with_self(q, s, slabs, _iota((N_HEADS, KVD), 0) < GROUP, ks_ref[...])

    win = win_ref[...].astype(BF16)
    o_win = _attend_with_self(q, _nn(q, win), [win], None, kw_ref[...])

    gt = gate_ref[...]
    pick = lambda branch: jnp.sum(jnp.where(_iota((N_HEADS, LANES), 1) == branch * N_HEADS + _iota((N_HEADS, LANES), 0), gt, 0.0),
                                  axis=-1, keepdims=True)
    o = pick(0) * ocmp_ref[...] + pick(1) * o_sel + pick(2) * o_win
    o_ref[...] = jnp.where(_own_value_lanes(), o, 0.0)


def _sel_sample(pt_flat, idx_flat, q_pad, pool, win, ks_new, kw_new, gates, o_cmp, *, layer, n_pages, n_pick):
    nb = q_pad.shape[0]
    wb = win.shape[-1]
    per_b = lambda i, pt, ix: (i, 0, 0)
    grid_spec = pltpu.PrefetchScalarGridSpec(
        num_scalar_prefetch=2, grid=(nb,),
        in_specs=[pl.BlockSpec((None, N_HEADS, KVD), per_b),
                  pl.BlockSpec(memory_space=pl.ANY),
                  pl.BlockSpec((None, None, KVD, wb), lambda i, pt, ix: (layer, i, 0, 0)),
                  pl.BlockSpec((None, 1, KVD), per_b), pl.BlockSpec((None, 1, KVD), per_b),
                  pl.BlockSpec((None, 1, LANES), per_b),
                  pl.BlockSpec((None, N_HEADS, KVD), per_b)],
        out_specs=pl.BlockSpec((None, N_HEADS, KVD), per_b),
        scratch_shapes=[pltpu.VMEM((2, N_KV, KVD, n_pick * PAGE_SIZE), F32), pltpu.SemaphoreType.DMA((2,))])
    return pl.pallas_call(
        functools.partial(_sel_sample_kernel, layer=layer, n_pages=n_pages, n_pick=n_pick),
        grid_spec=grid_spec,
        out_shape=jax.ShapeDtypeStruct((nb, N_HEADS, KVD), F32),
        compiler_params=_cparams("arbitrary"),
        name="sel_sample",
    )(pt_flat, idx_flat, q_pad, pool, win, ks_new, kw_new, gates, o_cmp)


def _swa_sample_kernel(q_ref, win_ref, kv_ref, sink_ref, o_ref):
    q = q_ref[...]
    win = win_ref[...].astype(BF16)
    o = _attend_with_self(q, _nn(q, win), [win], None, kv_ref[...], sink=sink_ref[...])
    o_ref[...] = jnp.where(_own_value_lanes(), o, 0.0)


def _swa_sample(q_pad, win, kv_new, sinks, *, layer):
    nb = q_pad.shape[0]
    wb = win.shape[-1]
    per_b = lambda i: (i, 0, 0)
    return pl.pallas_call(
        _swa_sample_kernel,
        grid=(nb,),
        in_specs=[pl.BlockSpec((None, N_HEADS, KVD), per_b),
                  pl.BlockSpec((None, None, KVD, wb), lambda i: (layer, i, 0, 0)),
                  pl.BlockSpec((None, 1, KVD), per_b),
                  pl.BlockSpec((N_HEADS, 1), lambda i: (0, 0))],
        out_specs=pl.BlockSpec((None, N_HEADS, KVD), per_b),
        out_shape=jax.ShapeDtypeStruct((nb, N_HEADS, KVD), F32),
        compiler_params=_cparams("parallel"),
        name="swa_sample",
    )(q_pad, win, kv_new, sinks.astype(F32).reshape(N_HEADS, 1))


def _pad_queries(q_heads, width):
    q = jnp.transpose(q_heads, (1, 0, 2))
    tiled = jnp.tile(q, (1, 1, width // HEAD_DIM))
    head_g = jnp.arange(N_HEADS)[:, None] // GROUP
    lane_g = jnp.arange(width)[None, :] // HEAD_DIM
    return jnp.where((head_g == lane_g)[None], tiled, jnp.zeros_like(tiled))


def _pool_view(cache):
    l, n_phys = cache.shape[:2]
    return jnp.transpose(cache, (0, 1, 3, 4, 5, 2)).reshape(l, n_phys, KVD, PAGE_SIZE)


def _state_view(state):
    l, nb, w = state.shape[:3]
    return jnp.transpose(state, (0, 1, 3, 4, 5, 2)).reshape(l, nb, KVD, w)


def _pad_w_out(w_out):
    d = w_out.shape[1]
    w = w_out.reshape(N_HEADS, 1, HEAD_DIM, d)
    head_g = jnp.arange(N_HEADS) // GROUP
    slot = jnp.arange(KVD // HEAD_DIM)
    keep = (slot[None, :] == head_g[:, None] + N_KV)[:, :, None, None]
    return jnp.where(keep, w, 0.0).reshape(N_HEADS * KVD, d).astype(BF16)


def _pe8(cmp_pe):
    flat = cmp_pe.reshape(2, 1, CMP_BLOCK * HEAD_DIM)
    return jnp.broadcast_to(flat, (2, 8, CMP_BLOCK * HEAD_DIM)).astype(BF16)


_NO_WBIG = lambda: jnp.zeros((1, 8, LANES), BF16)


def _nsa_prompt_mixer(h, g_pre, w_pad, wbig, pe8, cmp_w_bf, pos, *, tm):
    qn, qr, kvc_t, kvs_t, kvw_t, ks_bf, kw_bf, gates, cp = _project(h, g_pre, w_pad, pos, wbig, nsa=True, tm=tm)
    kcv = _cmpfin(cp, pe8, cmp_w_bf)
    o = _nsa_attention(qn, qr, kcv, ks_bf, kw_bf, gates)
    return o, kvc_t, kvs_t, kvw_t


def _swa_prompt_mixer(h, g_pre, w_pad, sinks, pos, *, tm):
    qr, kv_t, kv_bf = _project(h, g_pre, w_pad, pos, _NO_WBIG(), nsa=False, tm=tm)
    return _swa_attention(qr, kv_bf, sinks), kv_t


def _sample_pick_count(past_len):
    return min(N_SEL, past_len // SEL_BLOCK + 1) - 1


def _token_rows(t):
    return jnp.transpose(t[0], (1, 0))[:, None, :]


def _nsa_sample_mixer(hs, g_pre, w_pad, wbig, pe8, cmp_w_bf, pt_flat, pool_c, pool_s, win_a, *, layer, n_pages):
    nb = hs.shape[0]
    past_len = n_pages * PAGE_SIZE
    pos = jnp.full((nb,), past_len, jnp.int32)
    qn, qr, kvc_t, kvs_t, kvw_t, _, _, gates, _ = _project(hs[None], g_pre, w_pad, pos, wbig, nsa=True, tm=nb)
    n_pick = _sample_pick_count(past_len)
    o_cmp, imp = _cmp_sample(pt_flat, _pad_queries(qn[0], LANES), pool_c, wbig, pe8, cmp_w_bf,
                             layer=layer, n_pages=n_pages)
    idx = _topk_sample(imp.reshape(nb * N_KV, LANES), past_len=past_len, n_pick=n_pick)
    o = _sel_sample(pt_flat, idx[:, :n_pick].reshape(-1), _pad_queries(qr[0], KVD), pool_s, win_a,
                    _token_rows(kvs_t), _token_rows(kvw_t), jnp.transpose(gates, (1, 0, 2)), o_cmp,
                    layer=layer, n_pages=n_pages, n_pick=n_pick)
    return o.reshape(nb, N_HEADS * KVD), kvc_t[0], kvs_t[0], kvw_t[0]


def _swa_sample_mixer(hs, g_pre, w_pad, sinks, win_b, *, layer, past_len):
    nb = hs.shape[0]
    pos = jnp.full((nb,), past_len, jnp.int32)
    qr, kv_t, _ = _project(hs[None], g_pre, w_pad, pos, _NO_WBIG(), nsa=False, tm=nb)
    o = _swa_sample(_pad_queries(qr[0], KVD), win_b, _token_rows(kv_t), sinks, layer=layer)
    return o.reshape(nb, N_HEADS * KVD), kv_t[0]


def _pad_w_in(w):
    n = w.shape[1]
    n_pad = -(-n // LANES) * LANES
    return jnp.pad(w, ((0, 0), (0, n_pad - n))).astype(BF16)


def _wbig(cmp_w):
    cw = cmp_w.reshape(2, CMP_RATIO, CMP_STRIDE, HEAD_DIM, HEAD_DIM)
    eye = jnp.eye(N_KV, dtype=cmp_w.dtype)
    big = jnp.einsum('crsde,gy->csydrge', cw, eye)
    return big.reshape(2, CMP_STRIDE * LANES, CMP_RATIO * LANES).astype(BF16)


PROJ_TILE = 512
POST_TILE = 256


def _rows_from_slab(t):
    lead = t.shape[:-2]
    n = t.shape[-1]
    t = t.reshape(lead + (2, N_KV, HEAD_DIM, n))
    k = len(lead)
    return jnp.transpose(t, tuple(range(k)) + (k + 3, k, k + 1, k + 2))


def kernel(x_prompt, x_sample, page_table, cache_a_cmp, cache_a_sel, state_a_win, state_b_win, norms, ffn_w_gu, ffn_w_down, a_w_in, a_w_out, a_cmp_w, a_cmp_pe, b_w_in, b_w_out, b_sinks):
    b, s, d = x_prompt.shape
    nb, t, _ = x_sample.shape
    assert t == 1
    n_pages = page_table.shape[1]
    past_len = n_pages * PAGE_SIZE
    depth = norms.shape[0]
    pos = jnp.arange(s, dtype=jnp.int32)
    pt_flat = page_table.reshape(-1).astype(jnp.int32)
    pool_c, pool_s = _pool_view(cache_a_cmp), _pool_view(cache_a_sel)
    win_a, win_b = _state_view(state_a_win), _state_view(state_b_win)
    wa, wbw = win_a.shape[-1], win_b.shape[-1]

    hp, hs = x_prompt, x_sample.reshape(nb, d)
    outs = {k: [] for k in ("cmp_p", "cmp_s", "sel_p", "sel_s", "awin_p", "awin_s", "bwin_p", "bwin_s")}
    for i in range(depth):
        j = i // 2
        g_pre, g_post, f_pre, f_post = norms[i, 0], norms[i, 1], norms[i, 2], norms[i, 3]
        w_gu, w_down = ffn_w_gu[i].astype(BF16), ffn_w_down[i].astype(BF16)
        if i % 2 == 0:
            w_pad, wbig, pe8, cw = _pad_w_in(a_w_in[j]), _wbig(a_cmp_w[j]), _pe8(a_cmp_pe[j]), a_cmp_w[j].astype(BF16)
            op, c_p, s_p, w_p = _nsa_prompt_mixer(hp, g_pre, w_pad, wbig, pe8, cw, pos, tm=PROJ_TILE)
            os_, c_s, s_s, w_s = _nsa_sample_mixer(hs, g_pre, w_pad, wbig, pe8, cw, pt_flat, pool_c, pool_s, win_a,
                                                   layer=j, n_pages=n_pages)
            w_out = a_w_out[j]
            outs["cmp_p"].append(c_p)
            outs["cmp_s"].append(c_s)
            outs["sel_p"].append(s_p)
            outs["sel_s"].append(s_s)
            outs["awin_p"].append(w_p[:, :, s - min(A_WINDOW, s):])
            outs["awin_s"].append(jnp.concatenate([win_a[j], w_s[None].transpose(2, 1, 0)], axis=-1)[:, :, t:])
        else:
            w_pad = _pad_w_in(b_w_in[j])
            op, w_p = _swa_prompt_mixer(hp, g_pre, w_pad, b_sinks[j], pos, tm=PROJ_TILE)
            os_, w_s = _swa_sample_mixer(hs, g_pre, w_pad, b_sinks[j], win_b, layer=j, past_len=past_len)
            w_out = b_w_out[j]
            outs["bwin_p"].append(w_p[:, :, s - min(B_WINDOW, s):])
            outs["bwin_s"].append(jnp.concatenate([win_b[j], w_s[None].transpose(2, 1, 0)], axis=-1)[:, :, t:])
        hp = _post(hp.reshape(b * s, d), op.reshape(b * s, -1), w_out.astype(BF16), g_post, f_pre, f_post,
                   w_gu, w_down, tm=POST_TILE).reshape(b, s, d)
        hs = _post(hs, os_, _pad_w_out(w_out), g_post, f_pre, f_post, w_gu, w_down, tm=nb)

    stack = lambda key: _rows_from_slab(jnp.stack(outs[key]))
    new_tok = lambda key: jnp.transpose(jnp.stack(outs[key]).reshape(-1, 2, N_KV, HEAD_DIM, nb), (0, 4, 1, 2, 3))[:, :, None]
    return (hp, hs.reshape(nb, 1, d), stack("cmp_p"), new_tok("cmp_s"), stack("sel_p"), new_tok("sel_s"),
            stack("awin_p"), stack("awin_s"), stack("bwin_p"), stack("bwin_s"))
```

```python
import functools

import jax
import jax.numpy as jnp
from jax import lax
from jax.experimental import pallas as pl
from jax.experimental.pallas import tpu as pltpu

F32 = jnp.float32
BF16 = jnp.bfloat16

HEAD_DIM = 64
ROT_DIM = HEAD_DIM // 4
ROPE_THETA = 500000.0
RMS_EPS = 1e-6
N_HEADS = 16
N_KV = 2
GROUP = N_HEADS // N_KV
KVD = 2 * N_KV * HEAD_DIM
CMP_BLOCK = 32
CMP_STRIDE = 16
CMP_RATIO = CMP_BLOCK // CMP_STRIDE
SEL_BLOCK = 64
N_SEL = 16
A_WINDOW = 512
B_WINDOW = 128
FORCE_BONUS = 1000.0
PAGE_SIZE = 128
ATTN_SCALE = HEAD_DIM ** -0.5
LANES = 128
Q_TILE = 256
SEL_KEY_TILE = 512
CMP_COL_CHUNK = 128
NEG = -1e30
PICKED = -3e30
VMEM_LIMIT = 56 * 1024 * 1024


def _cparams(*sem):
    return pltpu.CompilerParams(dimension_semantics=sem, vmem_limit_bytes=VMEM_LIMIT)


def _rms(x, g):
    return x * lax.rsqrt(jnp.mean(x * x, axis=-1, keepdims=True) + RMS_EPS) * g


def _nt(a, b):
    return lax.dot_general(a, b, (((1,), (1,)), ((), ())), preferred_element_type=F32)


def _nn(a, b):
    return jnp.dot(a, b, preferred_element_type=F32)


def _split3(x):
    hi = x.astype(BF16)
    r1 = x - hi.astype(F32)
    mid = r1.astype(BF16)
    lo = (r1 - mid.astype(F32)).astype(BF16)
    return hi, mid, lo


def _rope_tables(pos):
    half = ROT_DIM // 2
    inv = ROPE_THETA ** (-jnp.arange(half, dtype=F32) * (2.0 / ROT_DIM))
    ang = pos.astype(F32)[:, None] * inv[None, :]
    cos, sin = jnp.cos(ang), jnp.sin(ang)
    t = pos.shape[0]
    c64 = jnp.concatenate([cos, cos, jnp.ones((t, HEAD_DIM - ROT_DIM), F32)], axis=-1)
    up64 = jnp.concatenate([jnp.zeros((t, half), F32), sin, jnp.zeros((t, HEAD_DIM - ROT_DIM), F32)], axis=-1)
    dn64 = jnp.concatenate([-sin, jnp.zeros((t, HEAD_DIM - half), F32)], axis=-1)
    two = lambda a: jnp.concatenate([a, a], axis=-1)
    return two(c64), two(up64), two(dn64)


CHUNK_PITCH = 24


def _chunk_products(rows_ref, wbig_ref, pitch=CMP_STRIDE):
    n_chunks = rows_ref.shape[1] // pitch
    acc = []
    for c in range(2):
        flat = [rows_ref[c, pl.ds(s, n_chunks, stride=pitch), :].astype(BF16) for s in range(CMP_STRIDE)]
        acc.append(_nn(jnp.concatenate(flat, axis=1), wbig_ref[c]))
    return jnp.concatenate([acc[c][:, r * LANES:(r + 1) * LANES] for r in range(CMP_RATIO) for c in range(2)], axis=-1)


def _proj_kernel(x_ref, g_ref, w_ref, c_ref, up_ref, dn_ref, wbig_ref, *refs, nsa):
    if nsa:
        (qn_ref, qr_ref, kvc_t_ref, kvs_t_ref, kvw_t_ref, ks_bf_ref, kw_bf_ref, gate_ref, cp_ref, kvc_scr) = refs
    else:
        (qr_ref, kv_t_ref, kv_bf_ref) = refs
    half = ROT_DIM // 2
    y = _rms(x_ref[...], g_ref[...])
    p = _nn(y.astype(BF16), w_ref[...])
    c, up, dn = c_ref[...], up_ref[...], dn_ref[...]

    def rope(v):
        return v * c + pltpu.roll(v, half, 1) * up + pltpu.roll(v, LANES - half, 1) * dn

    for j in range(N_HEADS // 2):
        ch = p[:, j * LANES:(j + 1) * LANES]
        rot = (rope(ch) * ATTN_SCALE).astype(BF16)
        for t in range(2):
            qr_ref[2 * j + t] = rot[:, t * HEAD_DIM:(t + 1) * HEAD_DIM]
        if nsa:
            raw = (ch * ATTN_SCALE).astype(BF16)
            for t in range(2):
                qn_ref[2 * j + t] = raw[:, t * HEAD_DIM:(t + 1) * HEAD_DIM]
    base = N_HEADS * HEAD_DIM

    def roped_kv(off):
        k = rope(p[:, off:off + LANES])
        v = p[:, off + LANES:off + KVD]
        return jnp.concatenate([k, v], axis=-1).T

    if nsa:
        kvc = p[:, base:base + KVD]
        kvc_t_ref[...] = kvc.T
        kvs_t = roped_kv(base + KVD)
        kvw_t = roped_kv(base + 2 * KVD)
        kvs_t_ref[...] = kvs_t
        kvw_t_ref[...] = kvw_t
        ks_bf_ref[...] = kvs_t.astype(BF16)
        kw_bf_ref[...] = kvw_t.astype(BF16)
        gate_ref[...] = jax.nn.sigmoid(p[:, base + 3 * KVD:base + 3 * KVD + LANES])
        for c in range(2):
            kvc_scr[c] = kvc[:, c * LANES:(c + 1) * LANES]
        cp_ref[...] = _chunk_products(kvc_scr, wbig_ref)
    else:
        kv_t = roped_kv(base)
        kv_t_ref[...] = kv_t
        kv_bf_ref[...] = kv_t.astype(BF16)


def _project(h, g_pre, w_pad, pos, wbig, *, nsa, tm):
    b, s, d = h.shape
    n = w_pad.shape[1]
    c, up, dn = _rope_tables(pos)
    grid = (b, s // tm)
    tok = lambda bb, i: (bb, i, 0)
    feat = lambda bb, i: (bb, 0, i)
    head = lambda bb, i: (bb, 0, i, 0)
    const2 = lambda bb, i: (0, 0)
    const3 = lambda bb, i: (0, 0, 0)
    tab = lambda bb, i: (i, 0)
    in_specs = [
        pl.BlockSpec((None, tm, d), tok),
        pl.BlockSpec((1, d), const2),
        pl.BlockSpec((d, n), const2),
        pl.BlockSpec((tm, LANES), tab),
        pl.BlockSpec((tm, LANES), tab),
        pl.BlockSpec((tm, LANES), tab),
        pl.BlockSpec(wbig.shape, lambda bb, i: (0,) * wbig.ndim),
    ]
    q_spec = pl.BlockSpec((None, N_HEADS, tm, HEAD_DIM), head)
    q_shape = jax.ShapeDtypeStruct((b, N_HEADS, s, HEAD_DIM), BF16)
    t_spec = pl.BlockSpec((None, KVD, tm), feat)
    t_f32 = jax.ShapeDtypeStruct((b, KVD, s), F32)
    t_bf = jax.ShapeDtypeStruct((b, KVD, s), BF16)
    if nsa:
        out_specs = [q_spec, q_spec, t_spec, t_spec, t_spec, t_spec, t_spec,
                     pl.BlockSpec((None, tm, LANES), tok),
                     pl.BlockSpec((None, tm // CMP_STRIDE, CMP_RATIO * KVD), tok)]
        out_shape = [q_shape, q_shape, t_f32, t_f32, t_f32, t_bf, t_bf,
                     jax.ShapeDtypeStruct((b, s, LANES), F32),
                     jax.ShapeDtypeStruct((b, s // CMP_STRIDE, CMP_RATIO * KVD), F32)]
        scratch = [pltpu.VMEM((2, tm, LANES), F32)]
    else:
        out_specs = [q_spec, t_spec, t_spec]
        out_shape = [q_shape, t_f32, t_bf]
        scratch = []
    return pl.pallas_call(
        functools.partial(_proj_kernel, nsa=nsa),
        grid=grid, in_specs=in_specs, out_specs=out_specs, out_shape=out_shape,
        scratch_shapes=scratch, compiler_params=_cparams("parallel", "parallel"),
        name="proj_nsa" if nsa else "proj_swa",
    )(h, g_pre.reshape(1, d), w_pad, c, up, dn, wbig)


def _post_kernel(h_ref, o_ref, wout_ref, gpost_ref, fpre_ref, fpost_ref, wgu_ref, wdown_ref, out_ref):
    d_ff = wdown_ref.shape[0]
    a = _nn(o_ref[...].astype(BF16), wout_ref[...])
    h = h_ref[...] + _rms(a, gpost_ref[...])
    x = _rms(h, fpre_ref[...]).astype(BF16)
    gu = _nn(x, wgu_ref[...])
    g, u = gu[:, :d_ff], gu[:, d_ff:]
    act = (g * jax.nn.sigmoid(g) * u).astype(BF16)
    out_ref[...] = h + _rms(_nn(act, wdown_ref[...]), fpost_ref[...])


def _post(h, o, w_out, g_post, f_pre, f_post, w_gu, w_down, *, tm):
    t, d = h.shape
    kd = o.shape[1]
    d_ff = w_down.shape[0]
    row = lambda i: (i, 0)
    const = lambda i: (0, 0)
    once = pl.Buffered(1)
    vec = lambda: pl.BlockSpec((1, d), const)
    return pl.pallas_call(
        _post_kernel,
        grid=(t // tm,),
        in_specs=[pl.BlockSpec((tm, d), row), pl.BlockSpec((tm, kd), row),
                  pl.BlockSpec((kd, d), const, pipeline_mode=once), vec(), vec(), vec(),
                  pl.BlockSpec((d, 2 * d_ff), const, pipeline_mode=once),
                  pl.BlockSpec((d_ff, d), const, pipeline_mode=once)],
        out_specs=pl.BlockSpec((tm, d), row),
        out_shape=jax.ShapeDtypeStruct((t, d), F32),
        compiler_params=_cparams("parallel"),
        name="post_ffn",
    )(h, o, w_out, g_post.reshape(1, d), f_pre.reshape(1, d), f_post.reshape(1, d), w_gu, w_down)


def _iota(shape, axis):
    return lax.broadcasted_iota(jnp.int32, shape, axis)


def _cmp_finish(cp, pe_ref, w_ref):
    ncp = cp.shape[0]
    out = []
    for c in range(2):
        pew = _nn(pe_ref[c], w_ref[c])[0:1]
        first = cp[:, c * LANES:(c + 1) * LANES]
        second = pltpu.roll(cp[:, KVD + c * LANES:KVD + (c + 1) * LANES], ncp - 1, 0)
        out.append(first + second + jnp.concatenate([pew, pew], axis=-1))
    return out


def _cmpfin_kernel(cp_ref, pe_ref, w_ref, out_ref):
    for c, both in enumerate(_cmp_finish(cp_ref[...], pe_ref, w_ref)):
        for g in range(N_KV):
            out_ref[c, g] = both[:, g * HEAD_DIM:(g + 1) * HEAD_DIM].astype(BF16)


def _cmpfin(cp, pe8, cmp_w_bf):
    b, ncp, _ = cp.shape
    return pl.pallas_call(
        _cmpfin_kernel,
        grid=(b,),
        in_specs=[pl.BlockSpec((None, ncp, CMP_RATIO * KVD), lambda i: (i, 0, 0)),
                  pl.BlockSpec(pe8.shape, lambda i: (0, 0, 0)),
                  pl.BlockSpec(cmp_w_bf.shape, lambda i: (0, 0, 0))],
        out_specs=pl.BlockSpec((None, 2, N_KV, ncp, HEAD_DIM), lambda i: (i, 0, 0, 0, 0)),
        out_shape=jax.ShapeDtypeStruct((b, 2, N_KV, ncp, HEAD_DIM), BF16),
        compiler_params=_cparams("parallel"),
        name="cmp_finish",
    )(cp, pe8, cmp_w_bf)


def _ones_rows(n):
    return jnp.ones((HEAD_DIM, n), BF16)


def _single(v_t):
    return jnp.concatenate([v_t, _ones_rows(v_t.shape[1])], axis=0)


def _dual(v_t):
    ones = _ones_rows(v_t.shape[1])
    return jnp.concatenate([v_t, ones, ones, v_t], axis=0)


def _lower_half(shape):
    return _iota(shape, 1) < HEAD_DIM


def _online_tile(q_aug, k_aug, v_aug, carry, mask=None):
    m, acc = carry
    tk = k_aug.shape[1]
    s = _nn(q_aug, k_aug).reshape(GROUP, Q_TILE, tk)
    if mask is not None:
        s = jnp.where(mask[None], s, NEG)
    m_new = jnp.maximum(m, jnp.max(s, axis=-1, keepdims=True))
    p = jnp.exp(s - m_new).reshape(GROUP * Q_TILE, tk).astype(BF16)
    return m_new, jnp.exp(m - m_new) * acc + _nt(p, v_aug).reshape(GROUP, Q_TILE, LANES)


def _window_pass(q, k_ref, v_ref, i, window):
    start = i * Q_TILE
    width = window + Q_TILE
    k0 = pl.multiple_of(jnp.maximum(start - window, 0), LANES)
    dist = (start + _iota((Q_TILE, width), 0)) - (k0 + _iota((Q_TILE, width), 1))
    s = _nn(q, k_ref[:, pl.ds(k0, width)]).reshape(GROUP, Q_TILE, width)
    s = jnp.where(((dist >= 0) & (dist <= window))[None], s, NEG)
    m = jnp.max(s, axis=-1, keepdims=True)
    p = jnp.exp(s - m).reshape(GROUP * Q_TILE, width).astype(BF16)
    return m, _nt(p, _dual(v_ref[:, pl.ds(k0, width)])).reshape(GROUP, Q_TILE, 2 * LANES)


def _pairs(fn):
    return jnp.concatenate([fn(2 * j, 2 * j + 1) for j in range(GROUP // 2)], axis=1)


def _nsa_attn_kernel(qn_ref, qr_ref, kc_ref, vc_ref, ks_ref, vs_ref, kw_ref, vw_ref, gate_ref, blk_ref, spread_ref,
                     o_ref, *, n_pick):
    g = pl.program_id(1)
    i = pl.program_id(2)
    start = i * Q_TILE
    rows = GROUP * Q_TILE
    ncp = kc_ref.shape[0]
    qn = qn_ref[...].reshape(rows, HEAD_DIM)
    qr = qr_ref[...].reshape(rows, HEAD_DIM)

    def cmp_branch(n_cols):
        s = _nt(qn, kc_ref[:n_cols, :]).reshape(GROUP, Q_TILE, n_cols)
        cend = _iota((Q_TILE, n_cols), 1) * CMP_STRIDE + (CMP_BLOCK - 1)
        cmask = (cend <= start + _iota((Q_TILE, n_cols), 0))[None]
        s = jnp.where(cmask, s, NEG)
        m = jnp.max(s, axis=-1, keepdims=True)
        e = jnp.exp(s - m)
        p = e * jnp.where(m > 0.5 * NEG, 1.0 / jnp.sum(e, axis=-1, keepdims=True), 0.0)
        vc = vc_ref[:n_cols, :]
        o = _nn(p.reshape(rows, n_cols).astype(BF16), jnp.concatenate([vc, vc], axis=1)).reshape(GROUP, Q_TILE, LANES)
        c0 = _iota((n_cols, LANES), 0) * CMP_STRIDE
        s0 = _iota((n_cols, LANES), 1) * SEL_BLOCK
        overlap = jnp.where((c0 < s0 + SEL_BLOCK) & (c0 + CMP_BLOCK > s0), 1.0, 0.0).astype(BF16)
        return o, sum(_nn(t, overlap) for t in _split3(jnp.sum(p, axis=0)))

    if ncp % CMP_COL_CHUNK == 0 and ncp > CMP_COL_CHUNK:
        reach = (start + Q_TILE - CMP_BLOCK) // CMP_STRIDE + 1
        n_var = ncp // CMP_COL_CHUNK
        which = jnp.minimum((reach + CMP_COL_CHUNK - 1) // CMP_COL_CHUNK, n_var) - 1
        o_cmp, imp = lax.switch(which, [functools.partial(cmp_branch, (v + 1) * CMP_COL_CHUNK) for v in range(n_var)])
    else:
        o_cmp, imp = cmp_branch(ncp)

    _, acc_win = _window_pass(qr, kw_ref, vw_ref, i, A_WINDOW)

    lane = _iota((Q_TILE, LANES), 1)
    lane_f = lane.astype(F32)
    qpos = start + _iota((Q_TILE, LANES), 0)
    qblk = qpos // SEL_BLOCK
    forced = (lane == 0) | (lane == qblk) | (lane == qblk - 1)
    allowed = lane * SEL_BLOCK <= qpos
    score = jnp.where(allowed, imp + jnp.where(forced, FORCE_BONUS, 0.0), NEG)
    off = jnp.full((Q_TILE, LANES), NEG, F32)
    for _ in range(n_pick):
        best = jnp.max(score, axis=-1, keepdims=True)
        pick = lane_f == jnp.min(jnp.where(score == best, lane_f, float(LANES)), axis=-1, keepdims=True)
        off = jnp.where(pick & allowed, 0.0, off)
        score = jnp.where(pick, PICKED, score)

    tk = SEL_KEY_TILE
    off_bf = off.astype(BF16)
    q_aug = jnp.concatenate([jnp.concatenate([off_bf] * GROUP, axis=0), qr], axis=1)

    def sel_tile(kt, carry, mask=None):
        k0 = pl.multiple_of(kt * tk, tk)
        k_aug = jnp.concatenate([blk_ref[:, pl.ds(k0, tk)], ks_ref[:, pl.ds(k0, tk)]], axis=0)
        return _online_tile(q_aug, k_aug, _single(vs_ref[:, pl.ds(k0, tk)]), carry, mask)

    last = start // tk
    init = (jnp.full((GROUP, Q_TILE, 1), NEG, F32), jnp.zeros((GROUP, Q_TILE, LANES), F32))
    carry = lax.fori_loop(0, last, sel_tile, init)
    causal = (last * tk + _iota((Q_TILE, tk), 1)) <= (start + _iota((Q_TILE, tk), 0))
    _, acc_sel = sel_tile(last, carry, causal)

    gt = gate_ref[...]
    g_hi = gt.astype(BF16)
    g_lo = (gt - g_hi.astype(F32)).astype(BF16)
    gates = _nn(g_hi, spread_ref[...]) + _nn(g_lo, spread_ref[...])
    lower = _lower_half((Q_TILE, LANES))
    width = GROUP * HEAD_DIM

    def cmp_pair(e, o):
        return jnp.where(lower, o_cmp[e], o_cmp[o])

    def sel_pair(e, o):
        a, b = acc_sel[e], acc_sel[o]
        return jnp.where(lower, a / pltpu.roll(a, HEAD_DIM, 1), pltpu.roll(b, HEAD_DIM, 1) / b)

    def win_pair(e, o):
        a, b = acc_win[e], acc_win[o]
        return jnp.where(lower, a[:, :LANES] / a[:, LANES:], b[:, LANES:] / b[:, :LANES])

    out = (gates[:, :width] * _pairs(cmp_pair) + gates[:, width:2 * width] * _pairs(sel_pair)
           + gates[:, 2 * width:] * _pairs(win_pair))
    o_ref[...] = out.astype(o_ref.dtype)


def _nsa_attention(qn, qr, kcv, ks_bf, kw_bf, gates):
    b, _, s, _ = qn.shape
    ncp = kcv.shape[3]
    assert s % SEL_KEY_TILE == 0 and s // SEL_BLOCK <= LANES and s >= A_WINDOW + Q_TILE
    blk_rows = (jnp.arange(LANES)[:, None] == (jnp.arange(s) // SEL_BLOCK)[None, :]).astype(BF16)
    col = jnp.arange(3 * GROUP * HEAD_DIM)
    src = (col // (GROUP * HEAD_DIM)) * N_HEADS + (col % (GROUP * HEAD_DIM)) // HEAD_DIM
    spread = (jnp.arange(LANES)[None, :, None] == (src[None, None, :] + GROUP * jnp.arange(N_KV)[:, None, None])).astype(BF16)
    q_spec = pl.BlockSpec((None, GROUP, Q_TILE, HEAD_DIM), lambda bb, g, i: (bb, g, i, 0))
    cmp_spec = lambda c: pl.BlockSpec((None, None, None, ncp, HEAD_DIM), lambda bb, g, i: (bb, c, g, 0, 0))
    slab = lambda c: pl.BlockSpec((None, HEAD_DIM, s), lambda bb, g, i: (bb, c * N_KV + g, 0))
    return pl.pallas_call(
        functools.partial(_nsa_attn_kernel, n_pick=min(N_SEL, s // SEL_BLOCK)),
        grid=(b, N_KV, s // Q_TILE),
        in_specs=[q_spec, q_spec, cmp_spec(0), cmp_spec(1), slab(0), slab(1), slab(0), slab(1),
                  pl.BlockSpec((None, Q_TILE, LANES), lambda bb, g, i: (bb, i, 0)),
                  pl.BlockSpec((LANES, s), lambda bb, g, i: (0, 0), pipeline_mode=pl.Buffered(1)),
                  pl.BlockSpec((None, LANES, 3 * GROUP * HEAD_DIM), lambda bb, g, i: (g, 0, 0))],
        out_specs=pl.BlockSpec((None, Q_TILE, GROUP * HEAD_DIM), lambda bb, g, i: (bb, i, g)),
        out_shape=jax.ShapeDtypeStruct((b, s, N_HEADS * HEAD_DIM), BF16),
        compiler_params=_cparams("parallel", "parallel", "arbitrary"),
        name="nsa_attn",
    )(qn, qr, kcv, kcv, ks_bf, ks_bf, kw_bf, kw_bf, gates, blk_rows, spread)


def _swa_attn_kernel(qr_ref, k_ref, v_ref, sink_ref, o_ref):
    i = pl.program_id(2)
    q = qr_ref[...].reshape(GROUP * Q_TILE, HEAD_DIM)
    m, acc = _window_pass(q, k_ref, v_ref, i, B_WINDOW)
    g = pl.program_id(1)
    lower = _lower_half((Q_TILE, LANES))

    def with_sink(head, num, den):
        sink = sink_ref[g, head]
        m_f = jnp.maximum(m[head], sink)
        scale = jnp.exp(m[head] - m_f)
        return num * scale / (den * scale + jnp.exp(sink - m_f))

    def pair(e, o):
        a, b = acc[e], acc[o]
        return jnp.where(lower, with_sink(e, a[:, :LANES], a[:, LANES:]), with_sink(o, b[:, LANES:], b[:, :LANES]))

    o_ref[...] = _pairs(pair).astype(o_ref.dtype)


def _swa_attention(qr, kv_bf, sinks):
    b, _, s, _ = qr.shape
    assert s >= B_WINDOW + Q_TILE
    slab = lambda c: pl.BlockSpec((None, HEAD_DIM, s), lambda bb, g, i: (bb, c * N_KV + g, 0))
    return pl.pallas_call(
        _swa_attn_kernel,
        grid=(b, N_KV, s // Q_TILE),
        in_specs=[pl.BlockSpec((None, GROUP, Q_TILE, HEAD_DIM), lambda bb, g, i: (bb, g, i, 0)),
                  slab(0), slab(1),
                  pl.BlockSpec(memory_space=pltpu.SMEM)],
        out_specs=pl.BlockSpec((None, Q_TILE, GROUP * HEAD_DIM), lambda bb, g, i: (bb, i, g)),
        out_shape=jax.ShapeDtypeStruct((b, s, N_HEADS * HEAD_DIM), BF16),
        compiler_params=_cparams("parallel", "parallel", "arbitrary"),
        name="swa_attn",
    )(qr, kv_bf, kv_bf, sinks.astype(F32).reshape(N_KV, GROUP))


def _own_value_lanes():
    row_g = _iota((N_HEADS, KVD), 0) // GROUP
    lane = _iota((N_HEADS, KVD), 1)
    return (lane >= KVD // 2) & ((lane - KVD // 2) // HEAD_DIM == row_g)


def _attend_with_self(q, s, k_t_bf_list, row_sel, self_row, sink=None):
    s_self = jnp.sum(q.astype(F32) * self_row, axis=-1, keepdims=True)
    m = jnp.maximum(jnp.max(s, axis=-1, keepdims=True), s_self)
    if sink is not None:
        m = jnp.maximum(m, sink)
    p = jnp.exp(s - m)
    p_self = jnp.exp(s_self - m)
    den = jnp.sum(p, axis=-1, keepdims=True) + p_self
    if sink is not None:
        den = den + jnp.exp(sink - m)
    p_bf = p.astype(BF16)
    pv = _nt(p_bf, k_t_bf_list[0])
    if len(k_t_bf_list) > 1:
        pv = jnp.where(row_sel, pv, _nt(p_bf, k_t_bf_list[1]))
    return (pv + p_self * self_row) / den


def _cmp_sample_kernel(pt_ref, qc_ref, pool_ref, wbig_ref, pe_ref, w_ref, o_ref, imp_ref, buf, xs, sem,
                       *, layer, n_pages, past_len):
    b = pl.program_id(0)
    slot = b % 2

    def fetch(bb, sl):
        def body(p, carry):
            page = pt_ref[bb * n_pages + p]
            pltpu.make_async_copy(pool_ref.at[layer, page], buf.at[sl, p], sem.at[sl]).start()
            return carry
        lax.fori_loop(0, n_pages, body, 0)

    @pl.when(b == 0)
    def _():
        fetch(0, 0)

    @pl.when(b + 1 < pl.num_programs(0))
    def _():
        fetch(b + 1, 1 - slot)

    def wait(p, carry):
        pltpu.make_async_copy(pool_ref.at[layer, 0], buf.at[slot, p], sem.at[slot]).wait()
        return carry
    lax.fori_loop(0, n_pages, wait, 0)

    per_page = PAGE_SIZE // CMP_STRIDE

    def flip(p, carry):
        r0 = pl.multiple_of(p * (per_page * CHUNK_PITCH), 8)
        for c in range(2):
            rows = buf[slot, p, c * LANES:(c + 1) * LANES, :].T
            for j in range(per_page):
                xs[c, pl.ds(r0 + j * CHUNK_PITCH, CMP_STRIDE), :] = rows[j * CMP_STRIDE:(j + 1) * CMP_STRIDE]
        return carry
    lax.fori_loop(0, n_pages, flip, 0, unroll=4)
    n_chunks = n_pages * per_page
    kc, vc = _cmp_finish(_chunk_products(xs, wbig_ref, CHUNK_PITCH), pe_ref, w_ref)

    q = qc_ref[...]
    s = _nt(q, kc.astype(BF16))
    cmask = _iota((N_HEADS, n_chunks), 1) * CMP_STRIDE + (CMP_BLOCK - 1) <= past_len
    s = jnp.where(cmask, s, NEG)
    m = jnp.max(s, axis=-1, keepdims=True)
    e = jnp.where(cmask, jnp.exp(s - m), 0.0)
    den = jnp.sum(e, axis=-1, keepdims=True)
    p = e / jnp.where(den > 0, den, 1.0)
    o = _nn(p.astype(BF16), vc.astype(BF16))
    own = _iota((N_HEADS, LANES), 0) // GROUP == _iota((N_HEADS, LANES), 1) // HEAD_DIM
    o_ref[...] = jnp.concatenate([jnp.zeros((N_HEADS, LANES), F32), jnp.where(own, o, 0.0)], axis=-1)

    c0 = _iota((n_chunks, LANES), 0) * CMP_STRIDE
    s0 = _iota((n_chunks, LANES), 1) * SEL_BLOCK
    overlap = jnp.where((c0 < s0 + SEL_BLOCK) & (c0 + CMP_BLOCK > s0), 1.0, 0.0).astype(BF16)
    imp_h = sum(_nn(t, overlap) for t in _split3(p))
    imp_ref[...] = jnp.concatenate(
        [jnp.sum(imp_h[g * GROUP:(g + 1) * GROUP], axis=0, keepdims=True) for g in range(N_KV)], axis=0)


def _topk_sample_kernel(imp_ref, idx_ref, *, past_len, n_pick):
    shape = imp_ref.shape
    lane = _iota(shape, 1)
    lane_f = lane.astype(F32)
    qblk = past_len // SEL_BLOCK
    forced = (lane == 0) | (lane == qblk) | (lane == qblk - 1)
    score = jnp.where(lane * SEL_BLOCK < past_len, imp_ref[...] + jnp.where(forced, FORCE_BONUS, 0.0), NEG)
    picks = jnp.zeros(shape, jnp.int32)
    for it in range(n_pick):
        best = jnp.max(score, axis=-1, keepdims=True)
        idx = jnp.min(jnp.where(score == best, lane_f, float(LANES)), axis=-1, keepdims=True)
        score = jnp.where(lane_f == idx, PICKED, score)
        picks = jnp.where(lane == it, idx.astype(jnp.int32), picks)
    idx_ref[...] = picks


def _topk_sample(imp, *, past_len, n_pick):
    rows = imp.shape[0]
    return pl.pallas_call(
        functools.partial(_topk_sample_kernel, past_len=past_len, n_pick=n_pick),
        grid=(1,),
        in_specs=[pl.BlockSpec((rows, LANES), lambda i: (0, 0))],
        out_specs=pl.BlockSpec((rows, LANES), lambda i: (0, 0)),
        out_shape=jax.ShapeDtypeStruct((rows, LANES), jnp.int32),
        compiler_params=_cparams("arbitrary"),
        name="topk_sample",
    )(imp)


def _cmp_sample(pt_flat, qc_pad, pool, wbig, pe8, cmp_w_bf, *, layer, n_pages):
    nb = qc_pad.shape[0]
    past_len = n_pages * PAGE_SIZE
    n_chunks = past_len // CMP_STRIDE
    assert past_len // SEL_BLOCK <= LANES
    const3 = lambda i, pt: (0, 0, 0)
    grid_spec = pltpu.PrefetchScalarGridSpec(
        num_scalar_prefetch=1, grid=(nb,),
        in_specs=[pl.BlockSpec((None, N_HEADS, LANES), lambda i, pt: (i, 0, 0)),
                  pl.BlockSpec(memory_space=pl.ANY),
                  pl.BlockSpec(wbig.shape, lambda i, pt: (0,) * wbig.ndim), pl.BlockSpec(pe8.shape, const3),
                  pl.BlockSpec(cmp_w_bf.shape, const3)],
        out_specs=[pl.BlockSpec((None, N_HEADS, KVD), lambda i, pt: (i, 0, 0)),
                   pl.BlockSpec((None, N_KV, LANES), lambda i, pt: (i, 0, 0))],
        scratch_shapes=[pltpu.VMEM((2, n_pages, KVD, PAGE_SIZE), F32),
                        pltpu.VMEM((2, n_chunks * CHUNK_PITCH, LANES), F32),
                        pltpu.SemaphoreType.DMA((2,))])
    return pl.pallas_call(
        functools.partial(_cmp_sample_kernel, layer=layer, n_pages=n_pages, past_len=past_len),
        grid_spec=grid_spec,
        out_shape=[jax.ShapeDtypeStruct((nb, N_HEADS, KVD), F32), jax.ShapeDtypeStruct((nb, N_KV, LANES), F32)],
        compiler_params=_cparams("arbitrary"),
        name="cmp_sample",
    )(pt_flat, qc_pad, pool, wbig, pe8, cmp_w_bf)


def _sel_sample_kernel(pt_ref, idx_ref, q_ref, pool_ref, win_ref, ks_ref, kw_ref, gate_ref, ocmp_ref, o_ref, buf, sem,
                       *, layer, n_pages, n_pick):
    b = pl.program_id(0)
    slot = b % 2
    nk = n_pick * PAGE_SIZE

    def block_id(bb, g, k):
        return idx_ref[(bb * N_KV + g) * n_pick + k]

    def fetch(bb, sl):
        for g in range(N_KV):
            for k in range(n_pick):
                page = pt_ref[bb * n_pages + block_id(bb, g, k) // (PAGE_SIZE // SEL_BLOCK)]
                pltpu.make_async_copy(pool_ref.at[layer, page], buf.at[sl, g, :, pl.ds(k * PAGE_SIZE, PAGE_SIZE)],
                                      sem.at[sl]).start()

    @pl.when(b == 0)
    def _():
        fetch(0, 0)

    @pl.when(b + 1 < pl.num_programs(0))
    def _():
        fetch(b + 1, 1 - slot)

    for g in range(N_KV):
        for k in range(n_pick):
            pltpu.make_async_copy(pool_ref.at[layer, 0], buf.at[slot, g, :, pl.ds(k * PAGE_SIZE, PAGE_SIZE)],
                                  sem.at[slot]).wait()

    q = q_ref[...]
    lane = _iota((1, nk), 1)
    valid = []
    for g in range(N_KV):
        half = jnp.zeros((1, nk), jnp.int32)
        for k in range(n_pick):
            half = jnp.where(lane // PAGE_SIZE == k, block_id(b, g, k) % (PAGE_SIZE // SEL_BLOCK), half)
        valid.append(jnp.where((lane // SEL_BLOCK) % (PAGE_SIZE // SEL_BLOCK) == half, 0.0, NEG))
    slabs = [buf[slot, g].astype(BF16) for g in range(N_KV)]
    first = _iota((N_HEADS, nk), 0) < GROUP
    s = jnp.where(first, _nn(q, slabs[0]) + valid[0], _nn(q, slabs[1]) + valid[1])
    o_sel = _attend_with_self(q, s, slabs, _iota((N_HEADS, KVD), 0) < GROUP, ks_ref[...])

    win = win_ref[...].astype(BF16)
    o_win = _attend_with_self(q, _nn(q, win), [win], None, kw_ref[...])

    gt = gate_ref[...]
    pick = lambda branch: jnp.sum(jnp.where(_iota((N_HEADS, LANES), 1) == branch * N_HEADS + _iota((N_HEADS, LANES), 0), gt, 0.0),
                                  axis=-1, keepdims=True)
    o = pick(0) * ocmp_ref[...] + pick(1) * o_sel + pick(2) * o_win
    o_ref[...] = jnp.where(_own_value_lanes(), o, 0.0)


def _sel_sample(pt_flat, idx_flat, q_pad, pool, win, ks_new, kw_new, gates, o_cmp, *, layer, n_pages, n_pick):
    nb = q_pad.shape[0]
    wb = win.shape[-1]
    per_b = lambda i, pt, ix: (i, 0, 0)
    grid_spec = pltpu.PrefetchScalarGridSpec(
        num_scalar_prefetch=2, grid=(nb,),
        in_specs=[pl.BlockSpec((None, N_HEADS, KVD), per_b),
                  pl.BlockSpec(memory_space=pl.ANY),
                  pl.BlockSpec((None, None, KVD, wb), lambda i, pt, ix: (layer, i, 0, 0)),
                  pl.BlockSpec((None, 1, KVD), per_b), pl.BlockSpec((None, 1, KVD), per_b),
                  pl.BlockSpec((None, 1, LANES), per_b),
                  pl.BlockSpec((None, N_HEADS, KVD), per_b)],
        out_specs=pl.BlockSpec((None, N_HEADS, KVD), per_b),
        scratch_shapes=[pltpu.VMEM((2, N_KV, KVD, n_pick * PAGE_SIZE), F32), pltpu.SemaphoreType.DMA((2,))])
    return pl.pallas_call(
        functools.partial(_sel_sample_kernel, layer=layer, n_pages=n_pages, n_pick=n_pick),
        grid_spec=grid_spec,
        out_shape=jax.ShapeDtypeStruct((nb, N_HEADS, KVD), F32),
        compiler_params=_cparams("arbitrary"),
        name="sel_sample",
    )(pt_flat, idx_flat, q_pad, pool, win, ks_new, kw_new, gates, o_cmp)


def _swa_sample_kernel(q_ref, win_ref, kv_ref, sink_ref, o_ref):
    q = q_ref[...]
    win = win_ref[...].astype(BF16)
    o = _attend_with_self(q, _nn(q, win), [win], None, kv_ref[...], sink=sink_ref[...])
    o_ref[...] = jnp.where(_own_value_lanes(), o, 0.0)


def _swa_sample(q_pad, win, kv_new, sinks, *, layer):
    nb = q_pad.shape[0]
    wb = win.shape[-1]
    per_b = lambda i: (i, 0, 0)
    return pl.pallas_call(
        _swa_sample_kernel,
        grid=(nb,),
        in_specs=[pl.BlockSpec((None, N_HEADS, KVD), per_b),
                  pl.BlockSpec((None, None, KVD, wb), lambda i: (layer, i, 0, 0)),
                  pl.BlockSpec((None, 1, KVD), per_b),
                  pl.BlockSpec((N_HEADS, 1), lambda i: (0, 0))],
        out_specs=pl.BlockSpec((None, N_HEADS, KVD), per_b),
        out_shape=jax.ShapeDtypeStruct((nb, N_HEADS, KVD), F32),
        compiler_params=_cparams("parallel"),
        name="swa_sample",
    )(q_pad, win, kv_new, sinks.astype(F32).reshape(N_HEADS, 1))


def _pad_queries(q_heads, width):
    q = jnp.transpose(q_heads, (1, 0, 2))
    tiled = jnp.tile(q, (1, 1, width // HEAD_DIM))
    head_g = jnp.arange(N_HEADS)[:, None] // GROUP
    lane_g = jnp.arange(width)[None, :] // HEAD_DIM
    return jnp.where((head_g == lane_g)[None], tiled, jnp.zeros_like(tiled))


def _pool_view(cache):
    l, n_phys = cache.shape[:2]
    return jnp.transpose(cache, (0, 1, 3, 4, 5, 2)).reshape(l, n_phys, KVD, PAGE_SIZE)


def _state_view(state):
    l, nb, w = state.shape[:3]
    return jnp.transpose(state, (0, 1, 3, 4, 5, 2)).reshape(l, nb, KVD, w)


def _pad_w_out(w_out):
    d = w_out.shape[1]
    w = w_out.reshape(N_HEADS, 1, HEAD_DIM, d)
    head_g = jnp.arange(N_HEADS) // GROUP
    slot = jnp.arange(KVD // HEAD_DIM)
    keep = (slot[None, :] == head_g[:, None] + N_KV)[:, :, None, None]
    return jnp.where(keep, w, 0.0).reshape(N_HEADS * KVD, d).astype(BF16)


def _pe8(cmp_pe):
    flat = cmp_pe.reshape(2, 1, CMP_BLOCK * HEAD_DIM)
    return jnp.broadcast_to(flat, (2, 8, CMP_BLOCK * HEAD_DIM)).astype(BF16)


_NO_WBIG = lambda: jnp.zeros((1, 8, LANES), BF16)


def _nsa_prompt_mixer(h, g_pre, w_pad, wbig, pe8, cmp_w_bf, pos, *, tm):
    qn, qr, kvc_t, kvs_t, kvw_t, ks_bf, kw_bf, gates, cp = _project(h, g_pre, w_pad, pos, wbig, nsa=True, tm=tm)
    kcv = _cmpfin(cp, pe8, cmp_w_bf)
    o = _nsa_attention(qn, qr, kcv, ks_bf, kw_bf, gates)
    return o, kvc_t, kvs_t, kvw_t


def _swa_prompt_mixer(h, g_pre, w_pad, sinks, pos, *, tm):
    qr, kv_t, kv_bf = _project(h, g_pre, w_pad, pos, _NO_WBIG(), nsa=False, tm=tm)
    return _swa_attention(qr, kv_bf, sinks), kv_t


def _sample_pick_count(past_len):
    return min(N_SEL, past_len // SEL_BLOCK + 1) - 1


def _token_rows(t):
    return jnp.transpose(t[0], (1, 0))[:, None, :]


def _nsa_sample_mixer(hs, g_pre, w_pad, wbig, pe8, cmp_w_bf, pt_flat, pool_c, pool_s, win_a, *, layer, n_pages):
    nb = hs.shape[0]
    past_len = n_pages * PAGE_SIZE
    pos = jnp.full((nb,), past_len, jnp.int32)
    qn, qr, kvc_t, kvs_t, kvw_t, _, _, gates, _ = _project(hs[None], g_pre, w_pad, pos, wbig, nsa=True, tm=nb)
    n_pick = _sample_pick_count(past_len)
    o_cmp, imp = _cmp_sample(pt_flat, _pad_queries(qn[0], LANES), pool_c, wbig, pe8, cmp_w_bf,
                             layer=layer, n_pages=n_pages)
    idx = _topk_sample(imp.reshape(nb * N_KV, LANES), past_len=past_len, n_pick=n_pick)
    o = _sel_sample(pt_flat, idx[:, :n_pick].reshape(-1), _pad_queries(qr[0], KVD), pool_s, win_a,
                    _token_rows(kvs_t), _token_rows(kvw_t), jnp.transpose(gates, (1, 0, 2)), o_cmp,
                    layer=layer, n_pages=n_pages, n_pick=n_pick)
    return o.reshape(nb, N_HEADS * KVD), kvc_t[0], kvs_t[0], kvw_t[0]


def _swa_sample_mixer(hs, g_pre, w_pad, sinks, win_b, *, layer, past_len):
    nb = hs.shape[0]
    pos = jnp.full((nb,), past_len, jnp.int32)
    qr, kv_t, _ = _project(hs[None], g_pre, w_pad, pos, _NO_WBIG(), nsa=False, tm=nb)
    o = _swa_sample(_pad_queries(qr[0], KVD), win_b, _token_rows(kv_t), sinks, layer=layer)
    return o.reshape(nb, N_HEADS * KVD), kv_t[0]


def _pad_w_in(w):
    n = w.shape[1]
    n_pad = -(-n // LANES) * LANES
    return jnp.pad(w, ((0, 0), (0, n_pad - n))).astype(BF16)


def _wbig(cmp_w):
    cw = cmp_w.reshape(2, CMP_RATIO, CMP_STRIDE, HEAD_DIM, HEAD_DIM)
    eye = jnp.eye(N_KV, dtype=cmp_w.dtype)
    big = jnp.einsum('crsde,gy->csydrge', cw, eye)
    return big.reshape(2, CMP_STRIDE * LANES, CMP_RATIO * LANES).astype(BF16)


PROJ_TILE = 512
POST_TILE = 256


def _rows_from_slab(t):
    lead = t.shape[:-2]
    n = t.shape[-1]
    t = t.reshape(lead + (2, N_KV, HEAD_DIM, n))
    k = len(lead)
    return jnp.transpose(t, tuple(range(k)) + (k + 3, k, k + 1, k + 2))


def kernel(x_prompt, x_sample, page_table, cache_a_cmp, cache_a_sel, state_a_win, state_b_win, norms, ffn_w_gu, ffn_w_down, a_w_in, a_w_out, a_cmp_w, a_cmp_pe, b_w_in, b_w_out, b_sinks):
    b, s, d = x_prompt.shape
    nb, t, _ = x_sample.shape
    assert t == 1
    n_pages = page_table.shape[1]
    past_len = n_pages * PAGE_SIZE
    depth = norms.shape[0]
    pos = jnp.arange(s, dtype=jnp.int32)
    pt_flat = page_table.reshape(-1).astype(jnp.int32)
    pool_c, pool_s = _pool_view(cache_a_cmp), _pool_view(cache_a_sel)
    win_a, win_b = _state_view(state_a_win), _state_view(state_b_win)
    wa, wbw = win_a.shape[-1], win_b.shape[-1]

    hp, hs = x_prompt, x_sample.reshape(nb, d)
    outs = {k: [] for k in ("cmp_p", "cmp_s", "sel_p", "sel_s", "awin_p", "awin_s", "bwin_p", "bwin_s")}
    for i in range(depth):
        j = i // 2
        g_pre, g_post, f_pre, f_post = norms[i, 0], norms[i, 1], norms[i, 2], norms[i, 3]
        w_gu, w_down = ffn_w_gu[i].astype(BF16), ffn_w_down[i].astype(BF16)
        if i % 2 == 0:
            w_pad, wbig, pe8, cw = _pad_w_in(a_w_in[j]), _wbig(a_cmp_w[j]), _pe8(a_cmp_pe[j]), a_cmp_w[j].astype(BF16)
            op, c_p, s_p, w_p = _nsa_prompt_mixer(hp, g_pre, w_pad, wbig, pe8, cw, pos, tm=PROJ_TILE)
            os_, c_s, s_s, w_s = _nsa_sample_mixer(hs, g_pre, w_pad, wbig, pe8, cw, pt_flat, pool_c, pool_s, win_a,
                                                   layer=j, n_pages=n_pages)
            w_out = a_w_out[j]
            outs["cmp_p"].append(c_p)
            outs["cmp_s"].append(c_s)
            outs["sel_p"].append(s_p)
            outs["sel_s"].append(s_s)
            outs["awin_p"].append(w_p[:, :, s - min(A_WINDOW, s):])
            outs["awin_s"].append(w_s)
        else:
            w_pad = _pad_w_in(b_w_in[j])
            op, w_p = _swa_prompt_mixer(hp, g_pre, w_pad, b_sinks[j], pos, tm=PROJ_TILE)
            os_, w_s = _swa_sample_mixer(hs, g_pre, w_pad, b_sinks[j], win_b, layer=j, past_len=past_len)
            w_out = b_w_out[j]
            outs["bwin_p"].append(w_p[:, :, s - min(B_WINDOW, s):])
            outs["bwin_s"].append(w_s)
        hp = _post(hp.reshape(b * s, d), op.reshape(b * s, -1), w_out.astype(BF16), g_post, f_pre, f_post,
                   w_gu, w_down, tm=POST_TILE).reshape(b, s, d)
        hs = _post(hs, os_, _pad_w_out(w_out), g_post, f_pre, f_post, w_gu, w_down, tm=nb)

    stack = lambda key: _rows_from_slab(jnp.stack(outs[key]))
    new_tok = lambda key: jnp.transpose(jnp.stack(outs[key]).reshape(-1, 2, N_KV, HEAD_DIM, nb), (0, 4, 1, 2, 3))[:, :, None]

    def shifted(win, key):
        new = jnp.transpose(jnp.stack(outs[key]), (0, 2, 1))[..., None]
        return _rows_from_slab(jnp.concatenate([win[..., t:], new], axis=-1))

    return (hp, hs.reshape(nb, 1, d), stack("cmp_p"), new_tok("cmp_s"), stack("sel_p"), new_tok("sel_s"),
            stack("awin_p"), shifted(win_a, "awin_s"), stack("bwin_p"), shifted(win_b, "bwin_s"))
```

```python
import functools

import jax
import jax.numpy as jnp
from jax import lax
from jax.experimental import pallas as pl
from jax.experimental.pallas import tpu as pltpu

F32 = jnp.float32
BF16 = jnp.bfloat16

HEAD_DIM = 64
ROT_DIM = HEAD_DIM // 4
ROPE_THETA = 500000.0
RMS_EPS = 1e-6
N_HEADS = 16
N_KV = 2
GROUP = N_HEADS // N_KV
KVD = 2 * N_KV * HEAD_DIM
CMP_BLOCK = 32
CMP_STRIDE = 16
CMP_RATIO = CMP_BLOCK // CMP_STRIDE
SEL_BLOCK = 64
N_SEL = 16
A_WINDOW = 512
B_WINDOW = 128
FORCE_BONUS = 1000.0
PAGE_SIZE = 128
ATTN_SCALE = HEAD_DIM ** -0.5
LANES = 128
Q_TILE = 256
SEL_KEY_TILE = 1024
CMP_COL_CHUNK = 128
NEG = -1e30
PICKED = -3e30
VMEM_LIMIT = 56 * 1024 * 1024


def _cparams(*sem):
    return pltpu.CompilerParams(dimension_semantics=sem, vmem_limit_bytes=VMEM_LIMIT)


def _rms(x, g):
    return x * lax.rsqrt(jnp.mean(x * x, axis=-1, keepdims=True) + RMS_EPS) * g


def _nt(a, b):
    return lax.dot_general(a, b, (((1,), (1,)), ((), ())), preferred_element_type=F32)


def _nn(a, b):
    return jnp.dot(a, b, preferred_element_type=F32)


def _split3(x):
    hi = x.astype(BF16)
    r1 = x - hi.astype(F32)
    mid = r1.astype(BF16)
    lo = (r1 - mid.astype(F32)).astype(BF16)
    return hi, mid, lo


def _rope_tables(pos):
    half = ROT_DIM // 2
    inv = ROPE_THETA ** (-jnp.arange(half, dtype=F32) * (2.0 / ROT_DIM))
    ang = pos.astype(F32)[:, None] * inv[None, :]
    cos, sin = jnp.cos(ang), jnp.sin(ang)
    t = pos.shape[0]
    c64 = jnp.concatenate([cos, cos, jnp.ones((t, HEAD_DIM - ROT_DIM), F32)], axis=-1)
    up64 = jnp.concatenate([jnp.zeros((t, half), F32), sin, jnp.zeros((t, HEAD_DIM - ROT_DIM), F32)], axis=-1)
    dn64 = jnp.concatenate([-sin, jnp.zeros((t, HEAD_DIM - half), F32)], axis=-1)
    two = lambda a: jnp.concatenate([a, a], axis=-1)
    return two(c64), two(up64), two(dn64)


CHUNK_PITCH = 24


def _chunk_products(rows_ref, wbig_ref, pitch=CMP_STRIDE):
    n_chunks = rows_ref.shape[1] // pitch
    acc = []
    for c in range(2):
        flat = [rows_ref[c, pl.ds(s, n_chunks, stride=pitch), :].astype(BF16) for s in range(CMP_STRIDE)]
        acc.append(_nn(jnp.concatenate(flat, axis=1), wbig_ref[c]))
    return jnp.concatenate([acc[c][:, r * LANES:(r + 1) * LANES] for r in range(CMP_RATIO) for c in range(2)], axis=-1)


def _proj_kernel(x_ref, g_ref, w_ref, c_ref, up_ref, dn_ref, wbig_ref, *refs, nsa):
    if nsa:
        (qn_ref, qr_ref, kvc_t_ref, kvs_t_ref, kvw_t_ref, ks_bf_ref, kw_bf_ref, gate_ref, cp_ref, kvc_scr) = refs
    else:
        (qr_ref, kv_t_ref, kv_bf_ref) = refs
    half = ROT_DIM // 2
    y = _rms(x_ref[...], g_ref[...])
    p = _nn(y.astype(BF16), w_ref[...])
    c, up, dn = c_ref[...], up_ref[...], dn_ref[...]

    def rope(v):
        return v * c + pltpu.roll(v, half, 1) * up + pltpu.roll(v, LANES - half, 1) * dn

    for j in range(N_HEADS // 2):
        ch = p[:, j * LANES:(j + 1) * LANES]
        rot = (rope(ch) * ATTN_SCALE).astype(BF16)
        for t in range(2):
            qr_ref[2 * j + t] = rot[:, t * HEAD_DIM:(t + 1) * HEAD_DIM]
        if nsa:
            raw = (ch * ATTN_SCALE).astype(BF16)
            for t in range(2):
                qn_ref[2 * j + t] = raw[:, t * HEAD_DIM:(t + 1) * HEAD_DIM]
    base = N_HEADS * HEAD_DIM

    def roped_kv(off):
        k = rope(p[:, off:off + LANES])
        v = p[:, off + LANES:off + KVD]
        return jnp.concatenate([k, v], axis=-1).T

    if nsa:
        kvc = p[:, base:base + KVD]
        kvc_t_ref[...] = kvc.T
        kvs_t = roped_kv(base + KVD)
        kvw_t = roped_kv(base + 2 * KVD)
        kvs_t_ref[...] = kvs_t
        kvw_t_ref[...] = kvw_t
        ks_bf_ref[...] = kvs_t.astype(BF16)
        kw_bf_ref[...] = kvw_t.astype(BF16)
        gate_ref[...] = jax.nn.sigmoid(p[:, base + 3 * KVD:base + 3 * KVD + LANES])
        for c in range(2):
            kvc_scr[c] = kvc[:, c * LANES:(c + 1) * LANES]
        cp_ref[...] = _chunk_products(kvc_scr, wbig_ref)
    else:
        kv_t = roped_kv(base)
        kv_t_ref[...] = kv_t
        kv_bf_ref[...] = kv_t.astype(BF16)


def _project(h, g_pre, w_pad, pos, wbig, *, nsa, tm):
    b, s, d = h.shape
    n = w_pad.shape[1]
    c, up, dn = _rope_tables(pos)
    grid = (b, s // tm)
    tok = lambda bb, i: (bb, i, 0)
    feat = lambda bb, i: (bb, 0, i)
    head = lambda bb, i: (bb, 0, i, 0)
    const2 = lambda bb, i: (0, 0)
    const3 = lambda bb, i: (0, 0, 0)
    tab = lambda bb, i: (i, 0)
    in_specs = [
        pl.BlockSpec((None, tm, d), tok),
        pl.BlockSpec((1, d), const2),
        pl.BlockSpec((d, n), const2),
        pl.BlockSpec((tm, LANES), tab),
        pl.BlockSpec((tm, LANES), tab),
        pl.BlockSpec((tm, LANES), tab),
        pl.BlockSpec(wbig.shape, lambda bb, i: (0,) * wbig.ndim),
    ]
    q_spec = pl.BlockSpec((None, N_HEADS, tm, HEAD_DIM), head)
    q_shape = jax.ShapeDtypeStruct((b, N_HEADS, s, HEAD_DIM), BF16)
    t_spec = pl.BlockSpec((None, KVD, tm), feat)
    t_f32 = jax.ShapeDtypeStruct((b, KVD, s), F32)
    t_bf = jax.ShapeDtypeStruct((b, KVD, s), BF16)
    if nsa:
        out_specs = [q_spec, q_spec, t_spec, t_spec, t_spec, t_spec, t_spec,
                     pl.BlockSpec((None, tm, LANES), tok),
                     pl.BlockSpec((None, tm // CMP_STRIDE, CMP_RATIO * KVD), tok)]
        out_shape = [q_shape, q_shape, t_f32, t_f32, t_f32, t_bf, t_bf,
                     jax.ShapeDtypeStruct((b, s, LANES), F32),
                     jax.ShapeDtypeStruct((b, s // CMP_STRIDE, CMP_RATIO * KVD), F32)]
        scratch = [pltpu.VMEM((2, tm, LANES), F32)]
    else:
        out_specs = [q_spec, t_spec, t_spec]
        out_shape = [q_shape, t_f32, t_bf]
        scratch = []
    return pl.pallas_call(
        functools.partial(_proj_kernel, nsa=nsa),
        grid=grid, in_specs=in_specs, out_specs=out_specs, out_shape=out_shape,
        scratch_shapes=scratch, compiler_params=_cparams("parallel", "parallel"),
        name="proj_nsa" if nsa else "proj_swa",
    )(h, g_pre.reshape(1, d), w_pad, c, up, dn, wbig)


def _post_kernel(h_ref, o_ref, wout_ref, gpost_ref, fpre_ref, fpost_ref, wgu_ref, wdown_ref, out_ref):
    d_ff = wdown_ref.shape[0]
    a = _nn(o_ref[...].astype(BF16), wout_ref[...])
    h = h_ref[...] + _rms(a, gpost_ref[...])
    x = _rms(h, fpre_ref[...]).astype(BF16)
    gu = _nn(x, wgu_ref[...])
    g, u = gu[:, :d_ff], gu[:, d_ff:]
    act = (g * jax.nn.sigmoid(g) * u).astype(BF16)
    out_ref[...] = h + _rms(_nn(act, wdown_ref[...]), fpost_ref[...])


def _post(h, o, w_out, g_post, f_pre, f_post, w_gu, w_down, *, tm):
    t, d = h.shape
    kd = o.shape[1]
    d_ff = w_down.shape[0]
    row = lambda i: (i, 0)
    const = lambda i: (0, 0)
    once = pl.Buffered(1)
    vec = lambda: pl.BlockSpec((1, d), const)
    return pl.pallas_call(
        _post_kernel,
        grid=(t // tm,),
        in_specs=[pl.BlockSpec((tm, d), row), pl.BlockSpec((tm, kd), row),
                  pl.BlockSpec((kd, d), const, pipeline_mode=once), vec(), vec(), vec(),
                  pl.BlockSpec((d, 2 * d_ff), const, pipeline_mode=once),
                  pl.BlockSpec((d_ff, d), const, pipeline_mode=once)],
        out_specs=pl.BlockSpec((tm, d), row),
        out_shape=jax.ShapeDtypeStruct((t, d), F32),
        compiler_params=_cparams("parallel"),
        name="post_ffn",
    )(h, o, w_out, g_post.reshape(1, d), f_pre.reshape(1, d), f_post.reshape(1, d), w_gu, w_down)


def _iota(shape, axis):
    return lax.broadcasted_iota(jnp.int32, shape, axis)


def _cmp_finish(cp, pe_ref, w_ref):
    ncp = cp.shape[0]
    out = []
    for c in range(2):
        pew = _nn(pe_ref[c], w_ref[c])[0:1]
        first = cp[:, c * LANES:(c + 1) * LANES]
        second = pltpu.roll(cp[:, KVD + c * LANES:KVD + (c + 1) * LANES], ncp - 1, 0)
        out.append(first + second + jnp.concatenate([pew, pew], axis=-1))
    return out


def _cmpfin_kernel(cp_ref, pe_ref, w_ref, out_ref):
    for c, both in enumerate(_cmp_finish(cp_ref[...], pe_ref, w_ref)):
        for g in range(N_KV):
            out_ref[c, g] = both[:, g * HEAD_DIM:(g + 1) * HEAD_DIM].astype(BF16)


def _cmpfin(cp, pe8, cmp_w_bf):
    b, ncp, _ = cp.shape
    return pl.pallas_call(
        _cmpfin_kernel,
        grid=(b,),
        in_specs=[pl.BlockSpec((None, ncp, CMP_RATIO * KVD), lambda i: (i, 0, 0)),
                  pl.BlockSpec(pe8.shape, lambda i: (0, 0, 0)),
                  pl.BlockSpec(cmp_w_bf.shape, lambda i: (0, 0, 0))],
        out_specs=pl.BlockSpec((None, 2, N_KV, ncp, HEAD_DIM), lambda i: (i, 0, 0, 0, 0)),
        out_shape=jax.ShapeDtypeStruct((b, 2, N_KV, ncp, HEAD_DIM), BF16),
        compiler_params=_cparams("parallel"),
        name="cmp_finish",
    )(cp, pe8, cmp_w_bf)


def _ones_rows(n):
    return jnp.ones((HEAD_DIM, n), BF16)


def _single(v_t):
    return jnp.concatenate([v_t, _ones_rows(v_t.shape[1])], axis=0)


def _dual(v_t):
    ones = _ones_rows(v_t.shape[1])
    return jnp.concatenate([v_t, ones, ones, v_t], axis=0)


def _lower_half(shape):
    return _iota(shape, 1) < HEAD_DIM


def _online_tile(q_aug, k_aug, v_aug, carry, mask=None):
    m, acc = carry
    tk = k_aug.shape[1]
    s = _nn(q_aug, k_aug).reshape(GROUP, Q_TILE, tk)
    if mask is not None:
        s = jnp.where(mask[None], s, NEG)
    m_new = jnp.maximum(m, jnp.max(s, axis=-1, keepdims=True))
    p = jnp.exp(s - m_new).reshape(GROUP * Q_TILE, tk).astype(BF16)
    return m_new, jnp.exp(m - m_new) * acc + _nt(p, v_aug).reshape(GROUP, Q_TILE, LANES)


def _window_pass(q, k_ref, v_ref, i, window):
    start = i * Q_TILE
    width = window + Q_TILE
    k0 = pl.multiple_of(jnp.maximum(start - window, 0), LANES)
    dist = (start + _iota((Q_TILE, width), 0)) - (k0 + _iota((Q_TILE, width), 1))
    s = _nn(q, k_ref[:, pl.ds(k0, width)]).reshape(GROUP, Q_TILE, width)
    s = jnp.where(((dist >= 0) & (dist <= window))[None], s, NEG)
    m = jnp.max(s, axis=-1, keepdims=True)
    p = jnp.exp(s - m).reshape(GROUP * Q_TILE, width).astype(BF16)
    return m, _nt(p, _dual(v_ref[:, pl.ds(k0, width)])).reshape(GROUP, Q_TILE, 2 * LANES)


def _pairs(fn):
    return jnp.concatenate([fn(2 * j, 2 * j + 1) for j in range(GROUP // 2)], axis=1)


def _nsa_attn_kernel(qn_ref, qr_ref, kc_ref, vc_ref, ks_ref, vs_ref, kw_ref, vw_ref, gate_ref, blk_ref, spread_ref,
                     o_ref, *, n_pick):
    i = pl.program_id(2)
    start = i * Q_TILE
    rows = GROUP * Q_TILE
    ncp = kc_ref.shape[0]
    qn = qn_ref[...].reshape(rows, HEAD_DIM)
    qr = qr_ref[...].reshape(rows, HEAD_DIM)

    def cmp_branch(n_cols):
        s = _nt(qn, kc_ref[:n_cols, :]).reshape(GROUP, Q_TILE, n_cols)
        cend = _iota((Q_TILE, n_cols), 1) * CMP_STRIDE + (CMP_BLOCK - 1)
        cmask = (cend <= start + _iota((Q_TILE, n_cols), 0))[None]
        s = jnp.where(cmask, s, NEG)
        m = jnp.max(s, axis=-1, keepdims=True)
        e = jnp.exp(s - m)
        p = e * jnp.where(m > 0.5 * NEG, 1.0 / jnp.sum(e, axis=-1, keepdims=True), 0.0)
        vc = vc_ref[:n_cols, :]
        o = _nn(p.reshape(rows, n_cols).astype(BF16), jnp.concatenate([vc, vc], axis=1)).reshape(GROUP, Q_TILE, LANES)
        c0 = _iota((n_cols, LANES), 0) * CMP_STRIDE
        s0 = _iota((n_cols, LANES), 1) * SEL_BLOCK
        overlap = jnp.where((c0 < s0 + SEL_BLOCK) & (c0 + CMP_BLOCK > s0), 1.0, 0.0).astype(BF16)
        return o, sum(_nn(t, overlap) for t in _split3(jnp.sum(p, axis=0)))

    if ncp % CMP_COL_CHUNK == 0 and ncp > CMP_COL_CHUNK:
        reach = (start + Q_TILE - CMP_BLOCK) // CMP_STRIDE + 1
        n_var = ncp // CMP_COL_CHUNK
        which = jnp.minimum((reach + CMP_COL_CHUNK - 1) // CMP_COL_CHUNK, n_var) - 1
        o_cmp, imp = lax.switch(which, [functools.partial(cmp_branch, (v + 1) * CMP_COL_CHUNK) for v in range(n_var)])
    else:
        o_cmp, imp = cmp_branch(ncp)

    _, acc_win = _window_pass(qr, kw_ref, vw_ref, i, A_WINDOW)

    lane = _iota((Q_TILE, LANES), 1)
    lane_f = lane.astype(F32)
    qpos = start + _iota((Q_TILE, LANES), 0)
    qblk = qpos // SEL_BLOCK
    forced = (lane == 0) | (lane == qblk) | (lane == qblk - 1)
    allowed = lane * SEL_BLOCK <= qpos
    score = jnp.where(allowed, imp + jnp.where(forced, FORCE_BONUS, 0.0), NEG)
    off = jnp.full((Q_TILE, LANES), NEG, F32)
    for _ in range(n_pick):
        best = jnp.max(score, axis=-1, keepdims=True)
        pick = lane_f == jnp.min(jnp.where(score == best, lane_f, float(LANES)), axis=-1, keepdims=True)
        off = jnp.where(pick & allowed, 0.0, off)
        score = jnp.where(pick, PICKED, score)

    tk = SEL_KEY_TILE
    off_bf = off.astype(BF16)
    q_aug = jnp.concatenate([jnp.concatenate([off_bf] * GROUP, axis=0), qr], axis=1)

    def sel_tile(kt, carry, mask=None):
        k0 = pl.multiple_of(kt * tk, tk)
        k_aug = jnp.concatenate([blk_ref[:, pl.ds(k0, tk)], ks_ref[:, pl.ds(k0, tk)]], axis=0)
        return _online_tile(q_aug, k_aug, _single(vs_ref[:, pl.ds(k0, tk)]), carry, mask)

    last = start // tk
    init = (jnp.full((GROUP, Q_TILE, 1), NEG, F32), jnp.zeros((GROUP, Q_TILE, LANES), F32))
    carry = lax.fori_loop(0, last, sel_tile, init)
    causal = (last * tk + _iota((Q_TILE, tk), 1)) <= (start + _iota((Q_TILE, tk), 0))
    _, acc_sel = sel_tile(last, carry, causal)

    gt = gate_ref[...]
    g_hi = gt.astype(BF16)
    g_lo = (gt - g_hi.astype(F32)).astype(BF16)
    gates = _nn(g_hi, spread_ref[...]) + _nn(g_lo, spread_ref[...])
    lower = _lower_half((Q_TILE, LANES))
    width = GROUP * HEAD_DIM

    def cmp_pair(e, o):
        return jnp.where(lower, o_cmp[e], o_cmp[o])

    def sel_pair(e, o):
        a, b = acc_sel[e], acc_sel[o]
        return jnp.where(lower, a / pltpu.roll(a, HEAD_DIM, 1), pltpu.roll(b, HEAD_DIM, 1) / b)

    def win_pair(e, o):
        a, b = acc_win[e], acc_win[o]
        return jnp.where(lower, a[:, :LANES] / a[:, LANES:], b[:, LANES:] / b[:, :LANES])

    out = (gates[:, :width] * _pairs(cmp_pair) + gates[:, width:2 * width] * _pairs(sel_pair)
           + gates[:, 2 * width:] * _pairs(win_pair))
    o_ref[...] = out.astype(o_ref.dtype)


def _nsa_attention(qn, qr, kcv, ks_bf, kw_bf, gates):
    b, _, s, _ = qn.shape
    ncp = kcv.shape[3]
    assert s % SEL_KEY_TILE == 0 and s // SEL_BLOCK <= LANES and s >= A_WINDOW + Q_TILE
    blk_rows = (jnp.arange(LANES)[:, None] == (jnp.arange(s) // SEL_BLOCK)[None, :]).astype(BF16)
    col = jnp.arange(3 * GROUP * HEAD_DIM)
    src = (col // (GROUP * HEAD_DIM)) * N_HEADS + (col % (GROUP * HEAD_DIM)) // HEAD_DIM
    spread = (jnp.arange(LANES)[None, :, None] == (src[None, None, :] + GROUP * jnp.arange(N_KV)[:, None, None])).astype(BF16)
    q_spec = pl.BlockSpec((None, GROUP, Q_TILE, HEAD_DIM), lambda bb, g, i: (bb, g, i, 0))
    cmp_spec = lambda c: pl.BlockSpec((None, None, None, ncp, HEAD_DIM), lambda bb, g, i: (bb, c, g, 0, 0))
    slab = lambda c: pl.BlockSpec((None, HEAD_DIM, s), lambda bb, g, i: (bb, c * N_KV + g, 0))
    return pl.pallas_call(
        functools.partial(_nsa_attn_kernel, n_pick=min(N_SEL, s // SEL_BLOCK)),
        grid=(b, N_KV, s // Q_TILE),
        in_specs=[q_spec, q_spec, cmp_spec(0), cmp_spec(1), slab(0), slab(1), slab(0), slab(1),
                  pl.BlockSpec((None, Q_TILE, LANES), lambda bb, g, i: (bb, i, 0)),
                  pl.BlockSpec((LANES, s), lambda bb, g, i: (0, 0), pipeline_mode=pl.Buffered(1)),
                  pl.BlockSpec((None, LANES, 3 * GROUP * HEAD_DIM), lambda bb, g, i: (g, 0, 0))],
        out_specs=pl.BlockSpec((None, Q_TILE, GROUP * HEAD_DIM), lambda bb, g, i: (bb, i, g)),
        out_shape=jax.ShapeDtypeStruct((b, s, N_HEADS * HEAD_DIM), BF16),
        compiler_params=_cparams("parallel", "parallel", "arbitrary"),
        name="nsa_attn",
    )(qn, qr, kcv, kcv, ks_bf, ks_bf, kw_bf, kw_bf, gates, blk_rows, spread)


def _swa_attn_kernel(qr_ref, k_ref, v_ref, sink_ref, o_ref):
    i = pl.program_id(2)
    q = qr_ref[...].reshape(GROUP * Q_TILE, HEAD_DIM)
    m, acc = _window_pass(q, k_ref, v_ref, i, B_WINDOW)
    g = pl.program_id(1)
    lower = _lower_half((Q_TILE, LANES))

    def with_sink(head, num, den):
        sink = sink_ref[g, head]
        m_f = jnp.maximum(m[head], sink)
        scale = jnp.exp(m[head] - m_f)
        return num * scale / (den * scale + jnp.exp(sink - m_f))

    def pair(e, o):
        a, b = acc[e], acc[o]
        return jnp.where(lower, with_sink(e, a[:, :LANES], a[:, LANES:]), with_sink(o, b[:, LANES:], b[:, :LANES]))

    o_ref[...] = _pairs(pair).astype(o_ref.dtype)


def _swa_attention(qr, kv_bf, sinks):
    b, _, s, _ = qr.shape
    assert s >= B_WINDOW + Q_TILE
    slab = lambda c: pl.BlockSpec((None, HEAD_DIM, s), lambda bb, g, i: (bb, c * N_KV + g, 0))
    return pl.pallas_call(
        _swa_attn_kernel,
        grid=(b, N_KV, s // Q_TILE),
        in_specs=[pl.BlockSpec((None, GROUP, Q_TILE, HEAD_DIM), lambda bb, g, i: (bb, g, i, 0)),
                  slab(0), slab(1),
                  pl.BlockSpec(memory_space=pltpu.SMEM)],
        out_specs=pl.BlockSpec((None, Q_TILE, GROUP * HEAD_DIM), lambda bb, g, i: (bb, i, g)),
        out_shape=jax.ShapeDtypeStruct((b, s, N_HEADS * HEAD_DIM), BF16),
        compiler_params=_cparams("parallel", "parallel", "arbitrary"),
        name="swa_attn",
    )(qr, kv_bf, kv_bf, sinks.astype(F32).reshape(N_KV, GROUP))


def _own_value_lanes():
    row_g = _iota((N_HEADS, KVD), 0) // GROUP
    lane = _iota((N_HEADS, KVD), 1)
    return (lane >= KVD // 2) & ((lane - KVD // 2) // HEAD_DIM == row_g)


def _attend_with_self(q, s, k_t_bf_list, row_sel, self_row, sink=None):
    s_self = jnp.sum(q.astype(F32) * self_row, axis=-1, keepdims=True)
    m = jnp.maximum(jnp.max(s, axis=-1, keepdims=True), s_self)
    if sink is not None:
        m = jnp.maximum(m, sink)
    p = jnp.exp(s - m)
    p_self = jnp.exp(s_self - m)
    den = jnp.sum(p, axis=-1, keepdims=True) + p_self
    if sink is not None:
        den = den + jnp.exp(sink - m)
    p_bf = p.astype(BF16)
    pv = _nt(p_bf, k_t_bf_list[0])
    if len(k_t_bf_list) > 1:
        pv = jnp.where(row_sel, pv, _nt(p_bf, k_t_bf_list[1]))
    return (pv + p_self * self_row) / den


def _cmp_sample_kernel(pt_ref, qc_ref, pool_ref, wbig_ref, pe_ref, w_ref, o_ref, imp_ref, buf, xs, sem,
                       *, layer, n_pages, past_len):
    b = pl.program_id(0)
    slot = b % 2

    def fetch(bb, sl):
        def body(p, carry):
            page = pt_ref[bb * n_pages + p]
            pltpu.make_async_copy(pool_ref.at[layer, page], buf.at[sl, p], sem.at[sl]).start()
            return carry
        lax.fori_loop(0, n_pages, body, 0)

    @pl.when(b == 0)
    def _():
        fetch(0, 0)

    @pl.when(b + 1 < pl.num_programs(0))
    def _():
        fetch(b + 1, 1 - slot)

    def wait(p, carry):
        pltpu.make_async_copy(pool_ref.at[layer, 0], buf.at[slot, p], sem.at[slot]).wait()
        return carry
    lax.fori_loop(0, n_pages, wait, 0)

    per_page = PAGE_SIZE // CMP_STRIDE

    def flip(p, carry):
        r0 = pl.multiple_of(p * (per_page * CHUNK_PITCH), 8)
        for c in range(2):
            rows = buf[slot, p, c * LANES:(c + 1) * LANES, :].astype(BF16).T.astype(F32)
            for j in range(per_page):
                xs[c, pl.ds(r0 + j * CHUNK_PITCH, CMP_STRIDE), :] = rows[j * CMP_STRIDE:(j + 1) * CMP_STRIDE]
        return carry
    lax.fori_loop(0, n_pages, flip, 0, unroll=4)
    n_chunks = n_pages * per_page
    kc, vc = _cmp_finish(_chunk_products(xs, wbig_ref, CHUNK_PITCH), pe_ref, w_ref)

    q = qc_ref[...]
    s = _nt(q, kc.astype(BF16))
    cmask = _iota((N_HEADS, n_chunks), 1) * CMP_STRIDE + (CMP_BLOCK - 1) <= past_len
    s = jnp.where(cmask, s, NEG)
    m = jnp.max(s, axis=-1, keepdims=True)
    e = jnp.where(cmask, jnp.exp(s - m), 0.0)
    den = jnp.sum(e, axis=-1, keepdims=True)
    p = e / jnp.where(den > 0, den, 1.0)
    o = _nn(p.astype(BF16), vc.astype(BF16))
    own = _iota((N_HEADS, LANES), 0) // GROUP == _iota((N_HEADS, LANES), 1) // HEAD_DIM
    o_ref[...] = jnp.concatenate([jnp.zeros((N_HEADS, LANES), F32), jnp.where(own, o, 0.0)], axis=-1)

    c0 = _iota((n_chunks, LANES), 0) * CMP_STRIDE
    s0 = _iota((n_chunks, LANES), 1) * SEL_BLOCK
    overlap = jnp.where((c0 < s0 + SEL_BLOCK) & (c0 + CMP_BLOCK > s0), 1.0, 0.0).astype(BF16)
    imp_h = sum(_nn(t, overlap) for t in _split3(p))
    imp_ref[...] = jnp.concatenate(
        [jnp.sum(imp_h[g * GROUP:(g + 1) * GROUP], axis=0, keepdims=True) for g in range(N_KV)], axis=0)


def _topk_sample_kernel(imp_ref, idx_ref, *, past_len, n_pick):
    shape = imp_ref.shape
    lane = _iota(shape, 1)
    lane_f = lane.astype(F32)
    qblk = past_len // SEL_BLOCK
    forced = (lane == 0) | (lane == qblk) | (lane == qblk - 1)
    score = jnp.where(lane * SEL_BLOCK < past_len, imp_ref[...] + jnp.where(forced, FORCE_BONUS, 0.0), NEG)
    picks = jnp.zeros(shape, jnp.int32)
    for it in range(n_pick):
        best = jnp.max(score, axis=-1, keepdims=True)
        idx = jnp.min(jnp.where(score == best, lane_f, float(LANES)), axis=-1, keepdims=True)
        score = jnp.where(lane_f == idx, PICKED, score)
        picks = jnp.where(lane == it, idx.astype(jnp.int32), picks)
    idx_ref[...] = picks


def _topk_sample(imp, *, past_len, n_pick):
    rows = imp.shape[0]
    return pl.pallas_call(
        functools.partial(_topk_sample_kernel, past_len=past_len, n_pick=n_pick),
        grid=(1,),
        in_specs=[pl.BlockSpec((rows, LANES), lambda i: (0, 0))],
        out_specs=pl.BlockSpec((rows, LANES), lambda i: (0, 0)),
        out_shape=jax.ShapeDtypeStruct((rows, LANES), jnp.int32),
        compiler_params=_cparams("arbitrary"),
        name="topk_sample",
    )(imp)


def _cmp_sample(pt_flat, qc_pad, pool, wbig, pe8, cmp_w_bf, *, layer, n_pages):
    nb = qc_pad.shape[0]
    past_len = n_pages * PAGE_SIZE
    n_chunks = past_len // CMP_STRIDE
    assert past_len // SEL_BLOCK <= LANES
    const3 = lambda i, pt: (0, 0, 0)
    grid_spec = pltpu.PrefetchScalarGridSpec(
        num_scalar_prefetch=1, grid=(nb,),
        in_specs=[pl.BlockSpec((None, N_HEADS, LANES), lambda i, pt: (i, 0, 0)),
                  pl.BlockSpec(memory_space=pl.ANY),
                  pl.BlockSpec(wbig.shape, lambda i, pt: (0,) * wbig.ndim), pl.BlockSpec(pe8.shape, const3),
                  pl.BlockSpec(cmp_w_bf.shape, const3)],
        out_specs=[pl.BlockSpec((None, N_HEADS, KVD), lambda i, pt: (i, 0, 0)),
                   pl.BlockSpec((None, N_KV, LANES), lambda i, pt: (i, 0, 0))],
        scratch_shapes=[pltpu.VMEM((2, n_pages, KVD, PAGE_SIZE), F32),
                        pltpu.VMEM((2, n_chunks * CHUNK_PITCH, LANES), F32),
                        pltpu.SemaphoreType.DMA((2,))])
    return pl.pallas_call(
        functools.partial(_cmp_sample_kernel, layer=layer, n_pages=n_pages, past_len=past_len),
        grid_spec=grid_spec,
        out_shape=[jax.ShapeDtypeStruct((nb, N_HEADS, KVD), F32), jax.ShapeDtypeStruct((nb, N_KV, LANES), F32)],
        compiler_params=_cparams("arbitrary"),
        name="cmp_sample",
    )(pt_flat, qc_pad, pool, wbig, pe8, cmp_w_bf)


def _sel_sample_kernel(pt_ref, idx_ref, q_ref, pool_ref, win_ref, ks_ref, kw_ref, gate_ref, ocmp_ref, o_ref, buf, sem,
                       *, layer, n_pages, n_pick):
    step = pl.program_id(0)
    slot = step % 2
    n_seq = q_ref.shape[0]
    nk = n_pick * PAGE_SIZE

    def block_id(seq, g, k):
        return idx_ref[(seq * N_KV + g) * n_pick + k]

    def copy(seq, j, g, k, sl):
        page = pt_ref[seq * n_pages + block_id(seq, g, k) // (PAGE_SIZE // SEL_BLOCK)]
        return pltpu.make_async_copy(pool_ref.at[layer, page], buf.at[sl, j, g, :, pl.ds(k * PAGE_SIZE, PAGE_SIZE)],
                                     sem.at[sl])

    def fetch(st, sl):
        for j in range(n_seq):
            for g in range(N_KV):
                for k in range(n_pick):
                    copy(st * n_seq + j, j, g, k, sl).start()

    @pl.when(step == 0)
    def _():
        fetch(0, 0)

    @pl.when(step + 1 < pl.num_programs(0))
    def _():
        fetch(step + 1, 1 - slot)

    for j in range(n_seq):
        for g in range(N_KV):
            for k in range(n_pick):
                copy(step * n_seq + j, j, g, k, slot).wait()

    lane = _iota((1, nk), 1)
    first = _iota((N_HEADS, nk), 0) < GROUP
    head = _iota((N_HEADS, LANES), 0)
    for j in range(n_seq):
        seq = step * n_seq + j
        q = q_ref[j]
        valid = []
        for g in range(N_KV):
            half = jnp.zeros((1, nk), jnp.int32)
            for k in range(n_pick):
                half = jnp.where(lane // PAGE_SIZE == k, block_id(seq, g, k) % (PAGE_SIZE // SEL_BLOCK), half)
            valid.append(jnp.where((lane // SEL_BLOCK) % (PAGE_SIZE // SEL_BLOCK) == half, 0.0, NEG))
        slabs = [buf[slot, j, g].astype(BF16) for g in range(N_KV)]
        s = jnp.where(first, _nn(q, slabs[0]) + valid[0], _nn(q, slabs[1]) + valid[1])
        o_sel = _attend_with_self(q, s, slabs, _iota((N_HEADS, KVD), 0) < GROUP, ks_ref[j])

        win = win_ref[j].astype(BF16)
        o_win = _attend_with_self(q, _nn(q, win), [win], None, kw_ref[j])

        gt = gate_ref[j]
        pick = lambda branch: jnp.sum(jnp.where(_iota((N_HEADS, LANES), 1) == branch * N_HEADS + head, gt, 0.0),
                                      axis=-1, keepdims=True)
        o = pick(0) * ocmp_ref[j] + pick(1) * o_sel + pick(2) * o_win
        o_ref[j] = jnp.where(_own_value_lanes(), o, 0.0)


SEL_SAMPLE_STEP = 2
SWA_SAMPLE_STEP = 8


def _sel_sample(pt_flat, idx_flat, q_pad, pool, win, ks_new, kw_new, gates, o_cmp, *, layer, n_pages, n_pick):
    nb = q_pad.shape[0]
    wb = win.shape[-1]
    n_seq = SEL_SAMPLE_STEP if nb % SEL_SAMPLE_STEP == 0 else 1
    per_b = lambda i, pt, ix: (i, 0, 0)
    grid_spec = pltpu.PrefetchScalarGridSpec(
        num_scalar_prefetch=2, grid=(nb // n_seq,),
        in_specs=[pl.BlockSpec((n_seq, N_HEADS, KVD), per_b),
                  pl.BlockSpec(memory_space=pl.ANY),
                  pl.BlockSpec((None, n_seq, KVD, wb), lambda i, pt, ix: (layer, i, 0, 0)),
                  pl.BlockSpec((n_seq, 1, KVD), per_b), pl.BlockSpec((n_seq, 1, KVD), per_b),
                  pl.BlockSpec((n_seq, 1, LANES), per_b),
                  pl.BlockSpec((n_seq, N_HEADS, KVD), per_b)],
        out_specs=pl.BlockSpec((n_seq, N_HEADS, KVD), per_b),
        scratch_shapes=[pltpu.VMEM((2, n_seq, N_KV, KVD, n_pick * PAGE_SIZE), F32), pltpu.SemaphoreType.DMA((2,))])
    return pl.pallas_call(
        functools.partial(_sel_sample_kernel, layer=layer, n_pages=n_pages, n_pick=n_pick),
        grid_spec=grid_spec,
        out_shape=jax.ShapeDtypeStruct((nb, N_HEADS, KVD), F32),
        compiler_params=_cparams("arbitrary"),
        name="sel_sample",
    )(pt_flat, idx_flat, q_pad, pool, win, ks_new, kw_new, gates, o_cmp)


def _swa_sample_kernel(q_ref, win_ref, kv_ref, sink_ref, o_ref):
    for j in range(q_ref.shape[0]):
        q = q_ref[j]
        win = win_ref[j].astype(BF16)
        o = _attend_with_self(q, _nn(q, win), [win], None, kv_ref[j], sink=sink_ref[...])
        o_ref[j] = jnp.where(_own_value_lanes(), o, 0.0)


def _swa_sample(q_pad, win, kv_new, sinks, *, layer):
    nb = q_pad.shape[0]
    wb = win.shape[-1]
    n_seq = SWA_SAMPLE_STEP if nb % SWA_SAMPLE_STEP == 0 else 1
    per_b = lambda i: (i, 0, 0)
    return pl.pallas_call(
        _swa_sample_kernel,
        grid=(nb // n_seq,),
        in_specs=[pl.BlockSpec((n_seq, N_HEADS, KVD), per_b),
                  pl.BlockSpec((None, n_seq, KVD, wb), lambda i: (layer, i, 0, 0)),
                  pl.BlockSpec((n_seq, 1, KVD), per_b),
                  pl.BlockSpec((N_HEADS, 1), lambda i: (0, 0))],
        out_specs=pl.BlockSpec((n_seq, N_HEADS, KVD), per_b),
        out_shape=jax.ShapeDtypeStruct((nb, N_HEADS, KVD), F32),
        compiler_params=_cparams("parallel"),
        name="swa_sample",
    )(q_pad, win, kv_new, sinks.astype(F32).reshape(N_HEADS, 1))


def _pad_queries(q_heads, width):
    q = jnp.transpose(q_heads, (1, 0, 2))
    tiled = jnp.tile(q, (1, 1, width // HEAD_DIM))
    head_g = jnp.arange(N_HEADS)[:, None] // GROUP
    lane_g = jnp.arange(width)[None, :] // HEAD_DIM
    return jnp.where((head_g == lane_g)[None], tiled, jnp.zeros_like(tiled))


def _pool_view(cache):
    l, n_phys = cache.shape[:2]
    return jnp.transpose(cache, (0, 1, 3, 4, 5, 2)).reshape(l, n_phys, KVD, PAGE_SIZE)


def _state_view(state):
    l, nb, w = state.shape[:3]
    return jnp.transpose(state, (0, 1, 3, 4, 5, 2)).reshape(l, nb, KVD, w)


def _pad_w_out(w_out):
    d = w_out.shape[1]
    w = w_out.reshape(N_HEADS, 1, HEAD_DIM, d)
    head_g = jnp.arange(N_HEADS) // GROUP
    slot = jnp.arange(KVD // HEAD_DIM)
    keep = (slot[None, :] == head_g[:, None] + N_KV)[:, :, None, None]
    return jnp.where(keep, w, 0.0).reshape(N_HEADS * KVD, d).astype(BF16)


def _pe8(cmp_pe):
    flat = cmp_pe.reshape(2, 1, CMP_BLOCK * HEAD_DIM)
    return jnp.broadcast_to(flat, (2, 8, CMP_BLOCK * HEAD_DIM)).astype(BF16)


_NO_WBIG = lambda: jnp.zeros((1, 8, LANES), BF16)


def _nsa_prompt_mixer(h, g_pre, w_pad, wbig, pe8, cmp_w_bf, pos, *, tm):
    qn, qr, kvc_t, kvs_t, kvw_t, ks_bf, kw_bf, gates, cp = _project(h, g_pre, w_pad, pos, wbig, nsa=True, tm=tm)
    kcv = _cmpfin(cp, pe8, cmp_w_bf)
    o = _nsa_attention(qn, qr, kcv, ks_bf, kw_bf, gates)
    return o, kvc_t, kvs_t, kvw_t


def _swa_prompt_mixer(h, g_pre, w_pad, sinks, pos, *, tm):
    qr, kv_t, kv_bf = _project(h, g_pre, w_pad, pos, _NO_WBIG(), nsa=False, tm=tm)
    return _swa_attention(qr, kv_bf, sinks), kv_t


def _sample_pick_count(past_len):
    return min(N_SEL, past_len // SEL_BLOCK + 1) - 1


def _token_rows(t):
    return jnp.transpose(t[0], (1, 0))[:, None, :]


def _nsa_sample_mixer(hs, g_pre, w_pad, wbig, pe8, cmp_w_bf, pt_flat, pool_c, pool_s, win_a, *, layer, n_pages):
    nb = hs.shape[0]
    past_len = n_pages * PAGE_SIZE
    pos = jnp.full((nb,), past_len, jnp.int32)
    qn, qr, kvc_t, kvs_t, kvw_t, _, _, gates, _ = _project(hs[None], g_pre, w_pad, pos, wbig, nsa=True, tm=nb)
    n_pick = _sample_pick_count(past_len)
    o_cmp, imp = _cmp_sample(pt_flat, _pad_queries(qn[0], LANES), pool_c, wbig, pe8, cmp_w_bf,
                             layer=layer, n_pages=n_pages)
    idx = _topk_sample(imp.reshape(nb * N_KV, LANES), past_len=past_len, n_pick=n_pick)
    o = _sel_sample(pt_flat, idx[:, :n_pick].reshape(-1), _pad_queries(qr[0], KVD), pool_s, win_a,
                    _token_rows(kvs_t), _token_rows(kvw_t), jnp.transpose(gates, (1, 0, 2)), o_cmp,
                    layer=layer, n_pages=n_pages, n_pick=n_pick)
    return o.reshape(nb, N_HEADS * KVD), kvc_t[0], kvs_t[0], kvw_t[0]


def _swa_sample_mixer(hs, g_pre, w_pad, sinks, win_b, *, layer, past_len):
    nb = hs.shape[0]
    pos = jnp.full((nb,), past_len, jnp.int32)
    qr, kv_t, _ = _project(hs[None], g_pre, w_pad, pos, _NO_WBIG(), nsa=False, tm=nb)
    o = _swa_sample(_pad_queries(qr[0], KVD), win_b, _token_rows(kv_t), sinks, layer=layer)
    return o.reshape(nb, N_HEADS * KVD), kv_t[0]


def _pad_w_in(w):
    n = w.shape[1]
    n_pad = -(-n // LANES) * LANES
    return jnp.pad(w, ((0, 0), (0, n_pad - n))).astype(BF16)


def _wbig(cmp_w):
    cw = cmp_w.reshape(2, CMP_RATIO, CMP_STRIDE, HEAD_DIM, HEAD_DIM)
    eye = jnp.eye(N_KV, dtype=cmp_w.dtype)
    big = jnp.einsum('crsde,gy->csydrge', cw, eye)
    return big.reshape(2, CMP_STRIDE * LANES, CMP_RATIO * LANES).astype(BF16)


PROJ_TILE = 512
POST_TILE = 256


def _rows_from_slab(t):
    lead = t.shape[:-2]
    n = t.shape[-1]
    t = t.reshape(lead + (2, N_KV, HEAD_DIM, n))
    k = len(lead)
    return jnp.transpose(t, tuple(range(k)) + (k + 3, k, k + 1, k + 2))


def kernel(x_prompt, x_sample, page_table, cache_a_cmp, cache_a_sel, state_a_win, state_b_win, norms, ffn_w_gu, ffn_w_down, a_w_in, a_w_out, a_cmp_w, a_cmp_pe, b_w_in, b_w_out, b_sinks):
    b, s, d = x_prompt.shape
    nb, t, _ = x_sample.shape
    assert t == 1
    n_pages = page_table.shape[1]
    past_len = n_pages * PAGE_SIZE
    depth = norms.shape[0]
    pos = jnp.arange(s, dtype=jnp.int32)
    pt_flat = page_table.reshape(-1).astype(jnp.int32)
    pool_c, pool_s = _pool_view(cache_a_cmp), _pool_view(cache_a_sel)
    win_a, win_b = _state_view(state_a_win), _state_view(state_b_win)
    wa, wbw = win_a.shape[-1], win_b.shape[-1]

    hp, hs = x_prompt, x_sample.reshape(nb, d)
    outs = {k: [] for k in ("cmp_p", "cmp_s", "sel_p", "sel_s", "awin_p", "awin_s", "bwin_p", "bwin_s")}
    for i in range(depth):
        j = i // 2
        g_pre, g_post, f_pre, f_post = norms[i, 0], norms[i, 1], norms[i, 2], norms[i, 3]
        w_gu, w_down = ffn_w_gu[i].astype(BF16), ffn_w_down[i].astype(BF16)
        if i % 2 == 0:
            w_pad, wbig, pe8, cw = _pad_w_in(a_w_in[j]), _wbig(a_cmp_w[j]), _pe8(a_cmp_pe[j]), a_cmp_w[j].astype(BF16)
            op, c_p, s_p, w_p = _nsa_prompt_mixer(hp, g_pre, w_pad, wbig, pe8, cw, pos, tm=PROJ_TILE)
            os_, c_s, s_s, w_s = _nsa_sample_mixer(hs, g_pre, w_pad, wbig, pe8, cw, pt_flat, pool_c, pool_s, win_a,
                                                   layer=j, n_pages=n_pages)
            w_out = a_w_out[j]
            outs["cmp_p"].append(c_p)
            outs["cmp_s"].append(c_s)
            outs["sel_p"].append(s_p)
            outs["sel_s"].append(s_s)
            outs["awin_p"].append(w_p[:, :, s - min(A_WINDOW, s):])
            outs["awin_s"].append(w_s)
        else:
            w_pad = _pad_w_in(b_w_in[j])
            op, w_p = _swa_prompt_mixer(hp, g_pre, w_pad, b_sinks[j], pos, tm=PROJ_TILE)
            os_, w_s = _swa_sample_mixer(hs, g_pre, w_pad, b_sinks[j], win_b, layer=j, past_len=past_len)
            w_out = b_w_out[j]
            outs["bwin_p"].append(w_p[:, :, s - min(B_WINDOW, s):])
            outs["bwin_s"].append(w_s)
        hp = _post(hp.reshape(b * s, d), op.reshape(b * s, -1), w_out.astype(BF16), g_post, f_pre, f_post,
                   w_gu, w_down, tm=POST_TILE).reshape(b, s, d)
        hs = _post(hs, os_, _pad_w_out(w_out), g_post, f_pre, f_post, w_gu, w_down, tm=nb)

    stack = lambda key: _rows_from_slab(jnp.stack(outs[key]))
    new_tok = lambda key: jnp.transpose(jnp.stack(outs[key]).reshape(-1, 2, N_KV, HEAD_DIM, nb), (0, 4, 1, 2, 3))[:, :, None]

    def shifted(win, key):
        new = jnp.transpose(jnp.stack(outs[key]), (0, 2, 1))[..., None]
        return _rows_from_slab(jnp.concatenate([win[..., t:], new], axis=-1))

    return (hp, hs.reshape(nb, 1, d), stack("cmp_p"), new_tok("cmp_s"), stack("sel_p"), new_tok("sel_s"),
            stack("awin_p"), shifted(win_a, "awin_s"), stack("bwin_p"), shifted(win_b, "bwin_s"))
```

```python
import functools

import jax
import jax.numpy as jnp
from jax import lax
from jax.experimental import pallas as pl
from jax.experimental.pallas import tpu as pltpu

F32 = jnp.float32
BF16 = jnp.bfloat16

HEAD_DIM = 64
ROT_DIM = HEAD_DIM // 4
ROPE_THETA = 500000.0
RMS_EPS = 1e-6
N_HEADS = 16
N_KV = 2
GROUP = N_HEADS // N_KV
KVD = 2 * N_KV * HEAD_DIM
CMP_BLOCK = 32
CMP_STRIDE = 16
CMP_RATIO = CMP_BLOCK // CMP_STRIDE
SEL_BLOCK = 64
N_SEL = 16
A_WINDOW = 512
B_WINDOW = 128
FORCE_BONUS = 1000.0
PAGE_SIZE = 128
ATTN_SCALE = HEAD_DIM ** -0.5
LANES = 128
Q_TILE = 256
SEL_KEY_TILE = 1024
CMP_COL_CHUNK = 128
NEG = -1e30
PICKED = -3e30
VMEM_LIMIT = 56 * 1024 * 1024


def _cparams(*sem):
    return pltpu.CompilerParams(dimension_semantics=sem, vmem_limit_bytes=VMEM_LIMIT)


def _rms(x, g):
    return x * lax.rsqrt(jnp.mean(x * x, axis=-1, keepdims=True) + RMS_EPS) * g


def _nt(a, b):
    return lax.dot_general(a, b, (((1,), (1,)), ((), ())), preferred_element_type=F32)


def _nn(a, b):
    return jnp.dot(a, b, preferred_element_type=F32)


def _split3(x):
    hi = x.astype(BF16)
    r1 = x - hi.astype(F32)
    mid = r1.astype(BF16)
    lo = (r1 - mid.astype(F32)).astype(BF16)
    return hi, mid, lo


def _rope_tables(pos):
    half = ROT_DIM // 2
    inv = ROPE_THETA ** (-jnp.arange(half, dtype=F32) * (2.0 / ROT_DIM))
    ang = pos.astype(F32)[:, None] * inv[None, :]
    cos, sin = jnp.cos(ang), jnp.sin(ang)
    t = pos.shape[0]
    c64 = jnp.concatenate([cos, cos, jnp.ones((t, HEAD_DIM - ROT_DIM), F32)], axis=-1)
    up64 = jnp.concatenate([jnp.zeros((t, half), F32), sin, jnp.zeros((t, HEAD_DIM - ROT_DIM), F32)], axis=-1)
    dn64 = jnp.concatenate([-sin, jnp.zeros((t, HEAD_DIM - half), F32)], axis=-1)
    two = lambda a: jnp.concatenate([a, a], axis=-1)
    return two(c64), two(up64), two(dn64)


CHUNK_PITCH = 24


def _chunk_products(rows_ref, wbig_ref, pitch=CMP_STRIDE):
    n_chunks = rows_ref.shape[1] // pitch
    acc = []
    for c in range(2):
        flat = [rows_ref[c, pl.ds(s, n_chunks, stride=pitch), :].astype(BF16) for s in range(CMP_STRIDE)]
        acc.append(_nn(jnp.concatenate(flat, axis=1), wbig_ref[c]))
    return jnp.concatenate([acc[c][:, r * LANES:(r + 1) * LANES] for r in range(CMP_RATIO) for c in range(2)], axis=-1)


def _proj_kernel(x_ref, g_ref, w_ref, c_ref, up_ref, dn_ref, wbig_ref, *refs, nsa):
    if nsa:
        (qn_ref, qr_ref, kvc_t_ref, kvs_t_ref, kvw_t_ref, ks_bf_ref, kw_bf_ref, gate_ref, cp_ref, kvc_scr) = refs
    else:
        (qr_ref, kv_t_ref, kv_bf_ref) = refs
    half = ROT_DIM // 2
    y = _rms(x_ref[...], g_ref[...])
    p = _nn(y.astype(BF16), w_ref[...])
    c, up, dn = c_ref[...], up_ref[...], dn_ref[...]

    def rope(v):
        return v * c + pltpu.roll(v, half, 1) * up + pltpu.roll(v, LANES - half, 1) * dn

    for j in range(N_HEADS // 2):
        ch = p[:, j * LANES:(j + 1) * LANES]
        rot = (rope(ch) * ATTN_SCALE).astype(BF16)
        for t in range(2):
            qr_ref[2 * j + t] = rot[:, t * HEAD_DIM:(t + 1) * HEAD_DIM]
        if nsa:
            raw = (ch * ATTN_SCALE).astype(BF16)
            for t in range(2):
                qn_ref[2 * j + t] = raw[:, t * HEAD_DIM:(t + 1) * HEAD_DIM]
    base = N_HEADS * HEAD_DIM

    def roped_kv(off):
        k = rope(p[:, off:off + LANES])
        v = p[:, off + LANES:off + KVD]
        return jnp.concatenate([k, v], axis=-1).T

    if nsa:
        kvc = p[:, base:base + KVD]
        kvc_t_ref[...] = kvc.T
        kvs_t = roped_kv(base + KVD)
        kvw_t = roped_kv(base + 2 * KVD)
        kvs_t_ref[...] = kvs_t
        kvw_t_ref[...] = kvw_t
        ks_bf_ref[...] = kvs_t.astype(BF16)
        kw_bf_ref[...] = kvw_t.astype(BF16)
        gate_ref[...] = jax.nn.sigmoid(p[:, base + 3 * KVD:base + 3 * KVD + LANES])
        for c in range(2):
            kvc_scr[c] = kvc[:, c * LANES:(c + 1) * LANES]
        cp_ref[...] = _chunk_products(kvc_scr, wbig_ref)
    else:
        kv_t = roped_kv(base)
        kv_t_ref[...] = kv_t
        kv_bf_ref[...] = kv_t.astype(BF16)


def _project(h, g_pre, w_pad, pos, wbig, *, nsa, tm):
    b, s, d = h.shape
    n = w_pad.shape[1]
    c, up, dn = _rope_tables(pos)
    grid = (b, s // tm)
    tok = lambda bb, i: (bb, i, 0)
    feat = lambda bb, i: (bb, 0, i)
    head = lambda bb, i: (bb, 0, i, 0)
    const2 = lambda bb, i: (0, 0)
    const3 = lambda bb, i: (0, 0, 0)
    tab = lambda bb, i: (i, 0)
    in_specs = [
        pl.BlockSpec((None, tm, d), tok),
        pl.BlockSpec((1, d), const2),
        pl.BlockSpec((d, n), const2),
        pl.BlockSpec((tm, LANES), tab),
        pl.BlockSpec((tm, LANES), tab),
        pl.BlockSpec((tm, LANES), tab),
        pl.BlockSpec(wbig.shape, lambda bb, i: (0,) * wbig.ndim),
    ]
    q_spec = pl.BlockSpec((None, N_HEADS, tm, HEAD_DIM), head)
    q_shape = jax.ShapeDtypeStruct((b, N_HEADS, s, HEAD_DIM), BF16)
    t_spec = pl.BlockSpec((None, KVD, tm), feat)
    t_f32 = jax.ShapeDtypeStruct((b, KVD, s), F32)
    t_bf = jax.ShapeDtypeStruct((b, KVD, s), BF16)
    if nsa:
        out_specs = [q_spec, q_spec, t_spec, t_spec, t_spec, t_spec, t_spec,
                     pl.BlockSpec((None, tm, LANES), tok),
                     pl.BlockSpec((None, tm // CMP_STRIDE, CMP_RATIO * KVD), tok)]
        out_shape = [q_shape, q_shape, t_f32, t_f32, t_f32, t_bf, t_bf,
                     jax.ShapeDtypeStruct((b, s, LANES), F32),
                     jax.ShapeDtypeStruct((b, s // CMP_STRIDE, CMP_RATIO * KVD), F32)]
        scratch = [pltpu.VMEM((2, tm, LANES), F32)]
    else:
        out_specs = [q_spec, t_spec, t_spec]
        out_shape = [q_shape, t_f32, t_bf]
        scratch = []
    return pl.pallas_call(
        functools.partial(_proj_kernel, nsa=nsa),
        grid=grid, in_specs=in_specs, out_specs=out_specs, out_shape=out_shape,
        scratch_shapes=scratch, compiler_params=_cparams("parallel", "parallel"),
        name="proj_nsa" if nsa else "proj_swa",
    )(h, g_pre.reshape(1, d), w_pad, c, up, dn, wbig)


def _post_kernel(h_ref, o_ref, wout_ref, gpost_ref, fpre_ref, fpost_ref, wgu_ref, wdown_ref, out_ref):
    d_ff = wdown_ref.shape[0]
    a = _nn(o_ref[...].astype(BF16), wout_ref[...])
    h = h_ref[...] + _rms(a, gpost_ref[...])
    x = _rms(h, fpre_ref[...]).astype(BF16)
    gu = _nn(x, wgu_ref[...])
    g, u = gu[:, :d_ff], gu[:, d_ff:]
    act = (g * jax.nn.sigmoid(g) * u).astype(BF16)
    out_ref[...] = h + _rms(_nn(act, wdown_ref[...]), fpost_ref[...])


def _post(h, o, w_out, g_post, f_pre, f_post, w_gu, w_down, *, tm):
    t, d = h.shape
    kd = o.shape[1]
    d_ff = w_down.shape[0]
    row = lambda i: (i, 0)
    const = lambda i: (0, 0)
    once = pl.Buffered(1)
    vec = lambda: pl.BlockSpec((1, d), const)
    return pl.pallas_call(
        _post_kernel,
        grid=(t // tm,),
        in_specs=[pl.BlockSpec((tm, d), row), pl.BlockSpec((tm, kd), row),
                  pl.BlockSpec((kd, d), const, pipeline_mode=once), vec(), vec(), vec(),
                  pl.BlockSpec((d, 2 * d_ff), const, pipeline_mode=once),
                  pl.BlockSpec((d_ff, d), const, pipeline_mode=once)],
        out_specs=pl.BlockSpec((tm, d), row),
        out_shape=jax.ShapeDtypeStruct((t, d), F32),
        compiler_params=_cparams("parallel"),
        name="post_ffn",
    )(h, o, w_out, g_post.reshape(1, d), f_pre.reshape(1, d), f_post.reshape(1, d), w_gu, w_down)


def _iota(shape, axis):
    return lax.broadcasted_iota(jnp.int32, shape, axis)


def _cmp_finish(cp, pe_ref, w_ref):
    ncp = cp.shape[0]
    out = []
    for c in range(2):
        pew = _nn(pe_ref[c], w_ref[c])[0:1]
        first = cp[:, c * LANES:(c + 1) * LANES]
        second = pltpu.roll(cp[:, KVD + c * LANES:KVD + (c + 1) * LANES], ncp - 1, 0)
        out.append(first + second + jnp.concatenate([pew, pew], axis=-1))
    return out


def _cmpfin_kernel(cp_ref, pe_ref, w_ref, out_ref):
    for c, both in enumerate(_cmp_finish(cp_ref[...], pe_ref, w_ref)):
        for g in range(N_KV):
            out_ref[c, g] = both[:, g * HEAD_DIM:(g + 1) * HEAD_DIM].astype(BF16)


def _cmpfin(cp, pe8, cmp_w_bf):
    b, ncp, _ = cp.shape
    return pl.pallas_call(
        _cmpfin_kernel,
        grid=(b,),
        in_specs=[pl.BlockSpec((None, ncp, CMP_RATIO * KVD), lambda i: (i, 0, 0)),
                  pl.BlockSpec(pe8.shape, lambda i: (0, 0, 0)),
                  pl.BlockSpec(cmp_w_bf.shape, lambda i: (0, 0, 0))],
        out_specs=pl.BlockSpec((None, 2, N_KV, ncp, HEAD_DIM), lambda i: (i, 0, 0, 0, 0)),
        out_shape=jax.ShapeDtypeStruct((b, 2, N_KV, ncp, HEAD_DIM), BF16),
        compiler_params=_cparams("parallel"),
        name="cmp_finish",
    )(cp, pe8, cmp_w_bf)


def _ones_rows(n):
    return jnp.ones((HEAD_DIM, n), BF16)


def _single(v_t):
    return jnp.concatenate([v_t, _ones_rows(v_t.shape[1])], axis=0)


def _dual(v_t):
    ones = _ones_rows(v_t.shape[1])
    return jnp.concatenate([v_t, ones, ones, v_t], axis=0)


def _lower_half(shape):
    return _iota(shape, 1) < HEAD_DIM


def _online_tile(q_aug, k_aug, v_aug, carry, mask=None):
    m, acc = carry
    tk = k_aug.shape[1]
    s = _nn(q_aug, k_aug).reshape(GROUP, Q_TILE, tk)
    if mask is not None:
        s = jnp.where(mask[None], s, NEG)
    m_new = jnp.maximum(m, jnp.max(s, axis=-1, keepdims=True))
    p = jnp.exp(s - m_new).reshape(GROUP * Q_TILE, tk).astype(BF16)
    return m_new, jnp.exp(m - m_new) * acc + _nt(p, v_aug).reshape(GROUP, Q_TILE, LANES)


def _window_pass(q, k_ref, v_ref, i, window):
    start = i * Q_TILE
    width = window + Q_TILE
    k0 = pl.multiple_of(jnp.maximum(start - window, 0), LANES)
    dist = (start + _iota((Q_TILE, width), 0)) - (k0 + _iota((Q_TILE, width), 1))
    s = _nn(q, k_ref[:, pl.ds(k0, width)]).reshape(GROUP, Q_TILE, width)
    s = jnp.where(((dist >= 0) & (dist <= window))[None], s, NEG)
    m = jnp.max(s, axis=-1, keepdims=True)
    p = jnp.exp(s - m).reshape(GROUP * Q_TILE, width).astype(BF16)
    return m, _nt(p, _dual(v_ref[:, pl.ds(k0, width)])).reshape(GROUP, Q_TILE, 2 * LANES)


def _pairs(fn):
    return jnp.concatenate([fn(2 * j, 2 * j + 1) for j in range(GROUP // 2)], axis=1)


def _nsa_attn_kernel(qn_ref, qr_ref, kc_ref, vc_ref, ks_ref, vs_ref, kw_ref, vw_ref, gate_ref, blk_ref, spread_ref,
                     o_ref, *, n_pick):
    i = pl.program_id(2)
    start = i * Q_TILE
    rows = GROUP * Q_TILE
    ncp = kc_ref.shape[0]
    qn = qn_ref[...].reshape(rows, HEAD_DIM)
    qr = qr_ref[...].reshape(rows, HEAD_DIM)

    def cmp_branch(n_cols):
        s = _nt(qn, kc_ref[:n_cols, :]).reshape(GROUP, Q_TILE, n_cols)
        cend = _iota((Q_TILE, n_cols), 1) * CMP_STRIDE + (CMP_BLOCK - 1)
        cmask = (cend <= start + _iota((Q_TILE, n_cols), 0))[None]
        s = jnp.where(cmask, s, NEG)
        m = jnp.max(s, axis=-1, keepdims=True)
        e = jnp.exp(s - m)
        p = e * jnp.where(m > 0.5 * NEG, 1.0 / jnp.sum(e, axis=-1, keepdims=True), 0.0)
        vc = vc_ref[:n_cols, :]
        o = _nn(p.reshape(rows, n_cols).astype(BF16), jnp.concatenate([vc, vc], axis=1)).reshape(GROUP, Q_TILE, LANES)
        c0 = _iota((n_cols, LANES), 0) * CMP_STRIDE
        s0 = _iota((n_cols, LANES), 1) * SEL_BLOCK
        overlap = jnp.where((c0 < s0 + SEL_BLOCK) & (c0 + CMP_BLOCK > s0), 1.0, 0.0).astype(BF16)
        return o, sum(_nn(t, overlap) for t in _split3(jnp.sum(p, axis=0)))

    if ncp % CMP_COL_CHUNK == 0 and ncp > CMP_COL_CHUNK:
        reach = (start + Q_TILE - CMP_BLOCK) // CMP_STRIDE + 1
        n_var = ncp // CMP_COL_CHUNK
        which = jnp.minimum((reach + CMP_COL_CHUNK - 1) // CMP_COL_CHUNK, n_var) - 1
        o_cmp, imp = lax.switch(which, [functools.partial(cmp_branch, (v + 1) * CMP_COL_CHUNK) for v in range(n_var)])
    else:
        o_cmp, imp = cmp_branch(ncp)

    _, acc_win = _window_pass(qr, kw_ref, vw_ref, i, A_WINDOW)

    lane = _iota((Q_TILE, LANES), 1)
    lane_f = lane.astype(F32)
    qpos = start + _iota((Q_TILE, LANES), 0)
    qblk = qpos // SEL_BLOCK
    forced = (lane == 0) | (lane == qblk) | (lane == qblk - 1)
    allowed = lane * SEL_BLOCK <= qpos
    score = jnp.where(allowed, imp + jnp.where(forced, FORCE_BONUS, 0.0), NEG)
    off = jnp.full((Q_TILE, LANES), NEG, F32)
    for _ in range(n_pick):
        best = jnp.max(score, axis=-1, keepdims=True)
        pick = lane_f == jnp.min(jnp.where(score == best, lane_f, float(LANES)), axis=-1, keepdims=True)
        off = jnp.where(pick & allowed, 0.0, off)
        score = jnp.where(pick, PICKED, score)

    tk = SEL_KEY_TILE
    off_bf = off.astype(BF16)
    q_aug = jnp.concatenate([jnp.concatenate([off_bf] * GROUP, axis=0), qr], axis=1)

    def sel_tile(kt, carry, mask=None):
        k0 = pl.multiple_of(kt * tk, tk)
        k_aug = jnp.concatenate([blk_ref[:, pl.ds(k0, tk)], ks_ref[:, pl.ds(k0, tk)]], axis=0)
        return _online_tile(q_aug, k_aug, _single(vs_ref[:, pl.ds(k0, tk)]), carry, mask)

    last = start // tk
    init = (jnp.full((GROUP, Q_TILE, 1), NEG, F32), jnp.zeros((GROUP, Q_TILE, LANES), F32))
    carry = lax.fori_loop(0, last, sel_tile, init)
    causal = (last * tk + _iota((Q_TILE, tk), 1)) <= (start + _iota((Q_TILE, tk), 0))
    _, acc_sel = sel_tile(last, carry, causal)

    gt = gate_ref[...]
    g_hi = gt.astype(BF16)
    g_lo = (gt - g_hi.astype(F32)).astype(BF16)
    gates = _nn(g_hi, spread_ref[...]) + _nn(g_lo, spread_ref[...])
    lower = _lower_half((Q_TILE, LANES))
    width = GROUP * HEAD_DIM

    def cmp_pair(e, o):
        return jnp.where(lower, o_cmp[e], o_cmp[o])

    def sel_pair(e, o):
        a, b = acc_sel[e], acc_sel[o]
        return jnp.where(lower, a / pltpu.roll(a, HEAD_DIM, 1), pltpu.roll(b, HEAD_DIM, 1) / b)

    def win_pair(e, o):
        a, b = acc_win[e], acc_win[o]
        return jnp.where(lower, a[:, :LANES] / a[:, LANES:], b[:, LANES:] / b[:, :LANES])

    out = (gates[:, :width] * _pairs(cmp_pair) + gates[:, width:2 * width] * _pairs(sel_pair)
           + gates[:, 2 * width:] * _pairs(win_pair))
    o_ref[...] = out.astype(o_ref.dtype)


def _nsa_attention(qn, qr, kcv, ks_bf, kw_bf, gates):
    b, _, s, _ = qn.shape
    ncp = kcv.shape[3]
    assert s % SEL_KEY_TILE == 0 and s // SEL_BLOCK <= LANES and s >= A_WINDOW + Q_TILE
    blk_rows = (jnp.arange(LANES)[:, None] == (jnp.arange(s) // SEL_BLOCK)[None, :]).astype(BF16)
    col = jnp.arange(3 * GROUP * HEAD_DIM)
    src = (col // (GROUP * HEAD_DIM)) * N_HEADS + (col % (GROUP * HEAD_DIM)) // HEAD_DIM
    spread = (jnp.arange(LANES)[None, :, None] == (src[None, None, :] + GROUP * jnp.arange(N_KV)[:, None, None])).astype(BF16)
    q_spec = pl.BlockSpec((None, GROUP, Q_TILE, HEAD_DIM), lambda bb, g, i: (bb, g, i, 0))
    cmp_spec = lambda c: pl.BlockSpec((None, None, None, ncp, HEAD_DIM), lambda bb, g, i: (bb, c, g, 0, 0))
    slab = lambda c: pl.BlockSpec((None, HEAD_DIM, s), lambda bb, g, i: (bb, c * N_KV + g, 0))
    return pl.pallas_call(
        functools.partial(_nsa_attn_kernel, n_pick=min(N_SEL, s // SEL_BLOCK)),
        grid=(b, N_KV, s // Q_TILE),
        in_specs=[q_spec, q_spec, cmp_spec(0), cmp_spec(1), slab(0), slab(1), slab(0), slab(1),
                  pl.BlockSpec((None, Q_TILE, LANES), lambda bb, g, i: (bb, i, 0)),
                  pl.BlockSpec((LANES, s), lambda bb, g, i: (0, 0), pipeline_mode=pl.Buffered(1)),
                  pl.BlockSpec((None, LANES, 3 * GROUP * HEAD_DIM), lambda bb, g, i: (g, 0, 0))],
        out_specs=pl.BlockSpec((None, Q_TILE, GROUP * HEAD_DIM), lambda bb, g, i: (bb, i, g)),
        out_shape=jax.ShapeDtypeStruct((b, s, N_HEADS * HEAD_DIM), BF16),
        compiler_params=_cparams("parallel", "parallel", "arbitrary"),
        name="nsa_attn",
    )(qn, qr, kcv, kcv, ks_bf, ks_bf, kw_bf, kw_bf, gates, blk_rows, spread)


def _swa_attn_kernel(qr_ref, k_ref, v_ref, sink_ref, o_ref):
    i = pl.program_id(2)
    q = qr_ref[...].reshape(GROUP * Q_TILE, HEAD_DIM)
    m, acc = _window_pass(q, k_ref, v_ref, i, B_WINDOW)
    g = pl.program_id(1)
    lower = _lower_half((Q_TILE, LANES))

    def with_sink(head, num, den):
        sink = sink_ref[g, head]
        m_f = jnp.maximum(m[head], sink)
        scale = jnp.exp(m[head] - m_f)
        return num * scale / (den * scale + jnp.exp(sink - m_f))

    def pair(e, o):
        a, b = acc[e], acc[o]
        return jnp.where(lower, with_sink(e, a[:, :LANES], a[:, LANES:]), with_sink(o, b[:, LANES:], b[:, :LANES]))

    o_ref[...] = _pairs(pair).astype(o_ref.dtype)


def _swa_attention(qr, kv_bf, sinks):
    b, _, s, _ = qr.shape
    assert s >= B_WINDOW + Q_TILE
    slab = lambda c: pl.BlockSpec((None, HEAD_DIM, s), lambda bb, g, i: (bb, c * N_KV + g, 0))
    return pl.pallas_call(
        _swa_attn_kernel,
        grid=(b, N_KV, s // Q_TILE),
        in_specs=[pl.BlockSpec((None, GROUP, Q_TILE, HEAD_DIM), lambda bb, g, i: (bb, g, i, 0)),
                  slab(0), slab(1),
                  pl.BlockSpec(memory_space=pltpu.SMEM)],
        out_specs=pl.BlockSpec((None, Q_TILE, GROUP * HEAD_DIM), lambda bb, g, i: (bb, i, g)),
        out_shape=jax.ShapeDtypeStruct((b, s, N_HEADS * HEAD_DIM), BF16),
        compiler_params=_cparams("parallel", "parallel", "arbitrary"),
        name="swa_attn",
    )(qr, kv_bf, kv_bf, sinks.astype(F32).reshape(N_KV, GROUP))


def _own_value_lanes():
    row_g = _iota((N_HEADS, KVD), 0) // GROUP
    lane = _iota((N_HEADS, KVD), 1)
    return (lane >= KVD // 2) & ((lane - KVD // 2) // HEAD_DIM == row_g)


def _attend_with_self(q, s, k_t_bf_list, row_sel, self_row, sink=None):
    s_self = jnp.sum(q.astype(F32) * self_row, axis=-1, keepdims=True)
    m = jnp.maximum(jnp.max(s, axis=-1, keepdims=True), s_self)
    if sink is not None:
        m = jnp.maximum(m, sink)
    p = jnp.exp(s - m)
    p_self = jnp.exp(s_self - m)
    den = jnp.sum(p, axis=-1, keepdims=True) + p_self
    if sink is not None:
        den = den + jnp.exp(sink - m)
    p_bf = p.astype(BF16)
    pv = _nt(p_bf, k_t_bf_list[0])
    if len(k_t_bf_list) > 1:
        pv = jnp.where(row_sel, pv, _nt(p_bf, k_t_bf_list[1]))
    return (pv + p_self * self_row) / den


def _cmp_sample_kernel(pt_ref, qc_ref, pool_ref, wbig_ref, pe_ref, w_ref, o_ref, imp_ref, buf, xs, sem,
                       *, layer, n_pages, past_len):
    b = pl.program_id(0)
    slot = b % 2

    def fetch(bb, sl):
        def body(p, carry):
            page = pt_ref[bb * n_pages + p]
            pltpu.make_async_copy(pool_ref.at[layer, page], buf.at[sl, p], sem.at[sl]).start()
            return carry
        lax.fori_loop(0, n_pages, body, 0)

    @pl.when(b == 0)
    def _():
        fetch(0, 0)

    @pl.when(b + 1 < pl.num_programs(0))
    def _():
        fetch(b + 1, 1 - slot)

    def wait(p, carry):
        pltpu.make_async_copy(pool_ref.at[layer, 0], buf.at[slot, p], sem.at[slot]).wait()
        return carry
    lax.fori_loop(0, n_pages, wait, 0)

    per_page = PAGE_SIZE // CMP_STRIDE

    def flip(p, carry):
        r0 = pl.multiple_of(p * (per_page * CHUNK_PITCH), 8)
        for c in range(2):
            rows = buf[slot, p, c * LANES:(c + 1) * LANES, :].astype(BF16).T.astype(F32)
            for j in range(per_page):
                xs[c, pl.ds(r0 + j * CHUNK_PITCH, CMP_STRIDE), :] = rows[j * CMP_STRIDE:(j + 1) * CMP_STRIDE]
        return carry
    lax.fori_loop(0, n_pages, flip, 0, unroll=8)
    n_chunks = n_pages * per_page
    kc, vc = _cmp_finish(_chunk_products(xs, wbig_ref, CHUNK_PITCH), pe_ref, w_ref)

    q = qc_ref[...]
    s = _nt(q, kc.astype(BF16))
    cmask = _iota((N_HEADS, n_chunks), 1) * CMP_STRIDE + (CMP_BLOCK - 1) <= past_len
    s = jnp.where(cmask, s, NEG)
    m = jnp.max(s, axis=-1, keepdims=True)
    e = jnp.where(cmask, jnp.exp(s - m), 0.0)
    den = jnp.sum(e, axis=-1, keepdims=True)
    p = e / jnp.where(den > 0, den, 1.0)
    o = _nn(p.astype(BF16), vc.astype(BF16))
    own = _iota((N_HEADS, LANES), 0) // GROUP == _iota((N_HEADS, LANES), 1) // HEAD_DIM
    o_ref[...] = jnp.concatenate([jnp.zeros((N_HEADS, LANES), F32), jnp.where(own, o, 0.0)], axis=-1)

    c0 = _iota((n_chunks, LANES), 0) * CMP_STRIDE
    s0 = _iota((n_chunks, LANES), 1) * SEL_BLOCK
    overlap = jnp.where((c0 < s0 + SEL_BLOCK) & (c0 + CMP_BLOCK > s0), 1.0, 0.0).astype(BF16)
    imp_h = sum(_nn(t, overlap) for t in _split3(p))
    imp_ref[...] = jnp.concatenate(
        [jnp.sum(imp_h[g * GROUP:(g + 1) * GROUP], axis=0, keepdims=True) for g in range(N_KV)], axis=0)


def _topk_sample_kernel(imp_ref, idx_ref, *, past_len, n_pick):
    shape = imp_ref.shape
    lane = _iota(shape, 1)
    lane_f = lane.astype(F32)
    qblk = past_len // SEL_BLOCK
    forced = (lane == 0) | (lane == qblk) | (lane == qblk - 1)
    score = jnp.where(lane * SEL_BLOCK < past_len, imp_ref[...] + jnp.where(forced, FORCE_BONUS, 0.0), NEG)
    picks = jnp.zeros(shape, jnp.int32)
    for it in range(n_pick):
        best = jnp.max(score, axis=-1, keepdims=True)
        idx = jnp.min(jnp.where(score == best, lane_f, float(LANES)), axis=-1, keepdims=True)
        score = jnp.where(lane_f == idx, PICKED, score)
        picks = jnp.where(lane == it, idx.astype(jnp.int32), picks)
    idx_ref[...] = picks


def _topk_sample(imp, *, past_len, n_pick):
    rows = imp.shape[0]
    return pl.pallas_call(
        functools.partial(_topk_sample_kernel, past_len=past_len, n_pick=n_pick),
        grid=(1,),
        in_specs=[pl.BlockSpec((rows, LANES), lambda i: (0, 0))],
        out_specs=pl.BlockSpec((rows, LANES), lambda i: (0, 0)),
        out_shape=jax.ShapeDtypeStruct((rows, LANES), jnp.int32),
        compiler_params=_cparams("arbitrary"),
        name="topk_sample",
    )(imp)


def _cmp_sample(pt_flat, qc_pad, pool, wbig, pe8, cmp_w_bf, *, layer, n_pages):
    nb = qc_pad.shape[0]
    past_len = n_pages * PAGE_SIZE
    n_chunks = past_len // CMP_STRIDE
    assert past_len // SEL_BLOCK <= LANES
    const3 = lambda i, pt: (0, 0, 0)
    grid_spec = pltpu.PrefetchScalarGridSpec(
        num_scalar_prefetch=1, grid=(nb,),
        in_specs=[pl.BlockSpec((None, N_HEADS, LANES), lambda i, pt: (i, 0, 0)),
                  pl.BlockSpec(memory_space=pl.ANY),
                  pl.BlockSpec(wbig.shape, lambda i, pt: (0,) * wbig.ndim), pl.BlockSpec(pe8.shape, const3),
                  pl.BlockSpec(cmp_w_bf.shape, const3)],
        out_specs=[pl.BlockSpec((None, N_HEADS, KVD), lambda i, pt: (i, 0, 0)),
                   pl.BlockSpec((None, N_KV, LANES), lambda i, pt: (i, 0, 0))],
        scratch_shapes=[pltpu.VMEM((2, n_pages, KVD, PAGE_SIZE), F32),
                        pltpu.VMEM((2, n_chunks * CHUNK_PITCH, LANES), F32),
                        pltpu.SemaphoreType.DMA((2,))])
    return pl.pallas_call(
        functools.partial(_cmp_sample_kernel, layer=layer, n_pages=n_pages, past_len=past_len),
        grid_spec=grid_spec,
        out_shape=[jax.ShapeDtypeStruct((nb, N_HEADS, KVD), F32), jax.ShapeDtypeStruct((nb, N_KV, LANES), F32)],
        compiler_params=_cparams("arbitrary"),
        name="cmp_sample",
    )(pt_flat, qc_pad, pool, wbig, pe8, cmp_w_bf)


def _sel_sample_kernel(pt_ref, idx_ref, q_ref, pool_ref, win_ref, ks_ref, kw_ref, gate_ref, ocmp_ref, o_ref, buf, sem,
                       *, layer, n_pages, n_pick):
    step = pl.program_id(0)
    slot = step % 2
    n_seq = q_ref.shape[0]
    nk = n_pick * PAGE_SIZE

    def block_id(seq, g, k):
        return idx_ref[(seq * N_KV + g) * n_pick + k]

    def copy(seq, j, g, k, sl):
        page = pt_ref[seq * n_pages + block_id(seq, g, k) // (PAGE_SIZE // SEL_BLOCK)]
        return pltpu.make_async_copy(pool_ref.at[layer, page], buf.at[sl, j, g, :, pl.ds(k * PAGE_SIZE, PAGE_SIZE)],
                                     sem.at[sl])

    def fetch(st, sl):
        for j in range(n_seq):
            for g in range(N_KV):
                for k in range(n_pick):
                    copy(st * n_seq + j, j, g, k, sl).start()

    @pl.when(step == 0)
    def _():
        fetch(0, 0)

    @pl.when(step + 1 < pl.num_programs(0))
    def _():
        fetch(step + 1, 1 - slot)

    for j in range(n_seq):
        for g in range(N_KV):
            for k in range(n_pick):
                copy(step * n_seq + j, j, g, k, slot).wait()

    lane = _iota((1, nk), 1)
    first = _iota((N_HEADS, nk), 0) < GROUP
    head = _iota((N_HEADS, LANES), 0)
    for j in range(n_seq):
        seq = step * n_seq + j
        q = q_ref[j]
        valid = []
        for g in range(N_KV):
            half = jnp.zeros((1, nk), jnp.int32)
            for k in range(n_pick):
                half = jnp.where(lane // PAGE_SIZE == k, block_id(seq, g, k) % (PAGE_SIZE // SEL_BLOCK), half)
            valid.append(jnp.where((lane // SEL_BLOCK) % (PAGE_SIZE // SEL_BLOCK) == half, 0.0, NEG))
        slabs = [buf[slot, j, g].astype(BF16) for g in range(N_KV)]
        s = jnp.where(first, _nn(q, slabs[0]) + valid[0], _nn(q, slabs[1]) + valid[1])
        o_sel = _attend_with_self(q, s, slabs, _iota((N_HEADS, KVD), 0) < GROUP, ks_ref[j])

        win = win_ref[j].astype(BF16)
        o_win = _attend_with_self(q, _nn(q, win), [win], None, kw_ref[j])

        gt = gate_ref[j]
        pick = lambda branch: jnp.sum(jnp.where(_iota((N_HEADS, LANES), 1) == branch * N_HEADS + head, gt, 0.0),
                                      axis=-1, keepdims=True)
        o = pick(0) * ocmp_ref[j] + pick(1) * o_sel + pick(2) * o_win
        o_ref[j] = jnp.where(_own_value_lanes(), o, 0.0)


SEL_SAMPLE_STEP = 2
SWA_SAMPLE_STEP = 8


def _sel_sample(pt_flat, idx_flat, q_pad, pool, win, ks_new, kw_new, gates, o_cmp, *, layer, n_pages, n_pick):
    nb = q_pad.shape[0]
    wb = win.shape[-1]
    n_seq = SEL_SAMPLE_STEP if nb % SEL_SAMPLE_STEP == 0 else 1
    per_b = lambda i, pt, ix: (i, 0, 0)
    grid_spec = pltpu.PrefetchScalarGridSpec(
        num_scalar_prefetch=2, grid=(nb // n_seq,),
        in_specs=[pl.BlockSpec((n_seq, N_HEADS, KVD), per_b),
                  pl.BlockSpec(memory_space=pl.ANY),
                  pl.BlockSpec((None, n_seq, KVD, wb), lambda i, pt, ix: (layer, i, 0, 0)),
                  pl.BlockSpec((n_seq, 1, KVD), per_b), pl.BlockSpec((n_seq, 1, KVD), per_b),
                  pl.BlockSpec((n_seq, 1, LANES), per_b),
                  pl.BlockSpec((n_seq, N_HEADS, KVD), per_b)],
        out_specs=pl.BlockSpec((n_seq, N_HEADS, KVD), per_b),
        scratch_shapes=[pltpu.VMEM((2, n_seq, N_KV, KVD, n_pick * PAGE_SIZE), F32), pltpu.SemaphoreType.DMA((2,))])
    return pl.pallas_call(
        functools.partial(_sel_sample_kernel, layer=layer, n_pages=n_pages, n_pick=n_pick),
        grid_spec=grid_spec,
        out_shape=jax.ShapeDtypeStruct((nb, N_HEADS, KVD), F32),
        compiler_params=_cparams("arbitrary"),
        name="sel_sample",
    )(pt_flat, idx_flat, q_pad, pool, win, ks_new, kw_new, gates, o_cmp)


def _swa_sample_kernel(q_ref, win_ref, kv_ref, sink_ref, o_ref):
    for j in range(q_ref.shape[0]):
        q = q_ref[j]
        win = win_ref[j].astype(BF16)
        o = _attend_with_self(q, _nn(q, win), [win], None, kv_ref[j], sink=sink_ref[...])
        o_ref[j] = jnp.where(_own_value_lanes(), o, 0.0)


def _swa_sample(q_pad, win, kv_new, sinks, *, layer):
    nb = q_pad.shape[0]
    wb = win.shape[-1]
    n_seq = SWA_SAMPLE_STEP if nb % SWA_SAMPLE_STEP == 0 else 1
    per_b = lambda i: (i, 0, 0)
    return pl.pallas_call(
        _swa_sample_kernel,
        grid=(nb // n_seq,),
        in_specs=[pl.BlockSpec((n_seq, N_HEADS, KVD), per_b),
                  pl.BlockSpec((None, n_seq, KVD, wb), lambda i: (layer, i, 0, 0)),
                  pl.BlockSpec((n_seq, 1, KVD), per_b),
                  pl.BlockSpec((N_HEADS, 1), lambda i: (0, 0))],
        out_specs=pl.BlockSpec((n_seq, N_HEADS, KVD), per_b),
        out_shape=jax.ShapeDtypeStruct((nb, N_HEADS, KVD), F32),
        compiler_params=_cparams("parallel"),
        name="swa_sample",
    )(q_pad, win, kv_new, sinks.astype(F32).reshape(N_HEADS, 1))


def _pad_queries(q_heads, width):
    q = jnp.transpose(q_heads, (1, 0, 2))
    tiled = jnp.tile(q, (1, 1, width // HEAD_DIM))
    head_g = jnp.arange(N_HEADS)[:, None] // GROUP
    lane_g = jnp.arange(width)[None, :] // HEAD_DIM
    return jnp.where((head_g == lane_g)[None], tiled, jnp.zeros_like(tiled))


def _pool_view(cache):
    l, n_phys = cache.shape[:2]
    return jnp.transpose(cache, (0, 1, 3, 4, 5, 2)).reshape(l, n_phys, KVD, PAGE_SIZE)


def _state_view(state):
    l, nb, w = state.shape[:3]
    return jnp.transpose(state, (0, 1, 3, 4, 5, 2)).reshape(l, nb, KVD, w)


def _pad_w_out(w_out):
    d = w_out.shape[1]
    w = w_out.reshape(N_HEADS, 1, HEAD_DIM, d)
    head_g = jnp.arange(N_HEADS) // GROUP
    slot = jnp.arange(KVD // HEAD_DIM)
    keep = (slot[None, :] == head_g[:, None] + N_KV)[:, :, None, None]
    return jnp.where(keep, w, 0.0).reshape(N_HEADS * KVD, d).astype(BF16)


def _pe8(cmp_pe):
    flat = cmp_pe.reshape(2, 1, CMP_BLOCK * HEAD_DIM)
    return jnp.broadcast_to(flat, (2, 8, CMP_BLOCK * HEAD_DIM)).astype(BF16)


_NO_WBIG = lambda: jnp.zeros((1, 8, LANES), BF16)


def _nsa_prompt_mixer(h, g_pre, w_pad, wbig, pe8, cmp_w_bf, pos, *, tm):
    qn, qr, kvc_t, kvs_t, kvw_t, ks_bf, kw_bf, gates, cp = _project(h, g_pre, w_pad, pos, wbig, nsa=True, tm=tm)
    kcv = _cmpfin(cp, pe8, cmp_w_bf)
    o = _nsa_attention(qn, qr, kcv, ks_bf, kw_bf, gates)
    return o, kvc_t, kvs_t, kvw_t


def _swa_prompt_mixer(h, g_pre, w_pad, sinks, pos, *, tm):
    qr, kv_t, kv_bf = _project(h, g_pre, w_pad, pos, _NO_WBIG(), nsa=False, tm=tm)
    return _swa_attention(qr, kv_bf, sinks), kv_t


def _sample_pick_count(past_len):
    return min(N_SEL, past_len // SEL_BLOCK + 1) - 1


def _token_rows(t):
    return jnp.transpose(t[0], (1, 0))[:, None, :]


def _nsa_sample_mixer(hs, g_pre, w_pad, wbig, pe8, cmp_w_bf, pt_flat, pool_c, pool_s, win_a, *, layer, n_pages):
    nb = hs.shape[0]
    past_len = n_pages * PAGE_SIZE
    pos = jnp.full((nb,), past_len, jnp.int32)
    qn, qr, kvc_t, kvs_t, kvw_t, _, _, gates, _ = _project(hs[None], g_pre, w_pad, pos, wbig, nsa=True, tm=nb)
    n_pick = _sample_pick_count(past_len)
    o_cmp, imp = _cmp_sample(pt_flat, _pad_queries(qn[0], LANES), pool_c, wbig, pe8, cmp_w_bf,
                             layer=layer, n_pages=n_pages)
    idx = _topk_sample(imp.reshape(nb * N_KV, LANES), past_len=past_len, n_pick=n_pick)
    o = _sel_sample(pt_flat, idx[:, :n_pick].reshape(-1), _pad_queries(qr[0], KVD), pool_s, win_a,
                    _token_rows(kvs_t), _token_rows(kvw_t), jnp.transpose(gates, (1, 0, 2)), o_cmp,
                    layer=layer, n_pages=n_pages, n_pick=n_pick)
    return o.reshape(nb, N_HEADS * KVD), kvc_t[0], kvs_t[0], kvw_t[0]


def _swa_sample_mixer(hs, g_pre, w_pad, sinks, win_b, *, layer, past_len):
    nb = hs.shape[0]
    pos = jnp.full((nb,), past_len, jnp.int32)
    qr, kv_t, _ = _project(hs[None], g_pre, w_pad, pos, _NO_WBIG(), nsa=False, tm=nb)
    o = _swa_sample(_pad_queries(qr[0], KVD), win_b, _token_rows(kv_t), sinks, layer=layer)
    return o.reshape(nb, N_HEADS * KVD), kv_t[0]


def _pad_w_in(w):
    n = w.shape[1]
    n_pad = -(-n // LANES) * LANES
    return jnp.pad(w, ((0, 0), (0, n_pad - n))).astype(BF16)


def _wbig(cmp_w):
    cw = cmp_w.reshape(2, CMP_RATIO, CMP_STRIDE, HEAD_DIM, HEAD_DIM)
    eye = jnp.eye(N_KV, dtype=cmp_w.dtype)
    big = jnp.einsum('crsde,gy->csydrge', cw, eye)
    return big.reshape(2, CMP_STRIDE * LANES, CMP_RATIO * LANES).astype(BF16)


PROJ_TILE = 512
POST_TILE = 512


def _rows_from_slab(t):
    lead = t.shape[:-2]
    n = t.shape[-1]
    t = t.reshape(lead + (2, N_KV, HEAD_DIM, n))
    k = len(lead)
    return jnp.transpose(t, tuple(range(k)) + (k + 3, k, k + 1, k + 2))


def kernel(x_prompt, x_sample, page_table, cache_a_cmp, cache_a_sel, state_a_win, state_b_win, norms, ffn_w_gu, ffn_w_down, a_w_in, a_w_out, a_cmp_w, a_cmp_pe, b_w_in, b_w_out, b_sinks):
    b, s, d = x_prompt.shape
    nb, t, _ = x_sample.shape
    assert t == 1
    n_pages = page_table.shape[1]
    past_len = n_pages * PAGE_SIZE
    depth = norms.shape[0]
    pos = jnp.arange(s, dtype=jnp.int32)
    pt_flat = page_table.reshape(-1).astype(jnp.int32)
    pool_c, pool_s = _pool_view(cache_a_cmp), _pool_view(cache_a_sel)
    win_a, win_b = _state_view(state_a_win), _state_view(state_b_win)
    wa, wbw = win_a.shape[-1], win_b.shape[-1]

    hp, hs = x_prompt, x_sample.reshape(nb, d)
    outs = {k: [] for k in ("cmp_p", "cmp_s", "sel_p", "sel_s", "awin_p", "awin_s", "bwin_p", "bwin_s")}
    for i in range(depth):
        j = i // 2
        g_pre, g_post, f_pre, f_post = norms[i, 0], norms[i, 1], norms[i, 2], norms[i, 3]
        w_gu, w_down = ffn_w_gu[i].astype(BF16), ffn_w_down[i].astype(BF16)
        if i % 2 == 0:
            w_pad, wbig, pe8, cw = _pad_w_in(a_w_in[j]), _wbig(a_cmp_w[j]), _pe8(a_cmp_pe[j]), a_cmp_w[j].astype(BF16)
            op, c_p, s_p, w_p = _nsa_prompt_mixer(hp, g_pre, w_pad, wbig, pe8, cw, pos, tm=PROJ_TILE)
            os_, c_s, s_s, w_s = _nsa_sample_mixer(hs, g_pre, w_pad, wbig, pe8, cw, pt_flat, pool_c, pool_s, win_a,
                                                   layer=j, n_pages=n_pages)
            w_out = a_w_out[j]
            outs["cmp_p"].append(c_p)
            outs["cmp_s"].append(c_s)
            outs["sel_p"].append(s_p)
            outs["sel_s"].append(s_s)
            outs["awin_p"].append(w_p[:, :, s - min(A_WINDOW, s):])
            outs["awin_s"].append(w_s)
        else:
            w_pad = _pad_w_in(b_w_in[j])
            op, w_p = _swa_prompt_mixer(hp, g_pre, w_pad, b_sinks[j], pos, tm=PROJ_TILE)
            os_, w_s = _swa_sample_mixer(hs, g_pre, w_pad, b_sinks[j], win_b, layer=j, past_len=past_len)
            w_out = b_w_out[j]
            outs["bwin_p"].append(w_p[:, :, s - min(B_WINDOW, s):])
            outs["bwin_s"].append(w_s)
        hp = _post(hp.reshape(b * s, d), op.reshape(b * s, -1), w_out.astype(BF16), g_post, f_pre, f_post,
                   w_gu, w_down, tm=POST_TILE).reshape(b, s, d)
        hs = _post(hs, os_, _pad_w_out(w_out), g_post, f_pre, f_post, w_gu, w_down, tm=nb)

    stack = lambda key: _rows_from_slab(jnp.stack(outs[key]))
    new_tok = lambda key: jnp.transpose(jnp.stack(outs[key]).reshape(-1, 2, N_KV, HEAD_DIM, nb), (0, 4, 1, 2, 3))[:, :, None]

    def shifted(win, key):
        new = jnp.transpose(jnp.stack(outs[key]), (0, 2, 1))[..., None]
        return _rows_from_slab(jnp.concatenate([win[..., t:], new], axis=-1))

    return (hp, hs.reshape(nb, 1, d), stack("cmp_p"), new_tok("cmp_s"), stack("sel_p"), new_tok("sel_s"),
            stack("awin_p"), shifted(win_a, "awin_s"), stack("bwin_p"), shifted(win_b, "bwin_s"))
```
